```python
import jax, jax.numpy as jnp
from jax import lax
import numpy as np

D_MODEL = 2048
BATCH = 2
SEQ = 16384
DEPTH = 2

N_MIXERS = 2
N_POOL_LAYERS = (DEPTH + 1) // 2
N_DSA_LAYERS = DEPTH // 2
POOL_WINDOWS = (2, 4, 8, 16)
N_POOL_GROUPS = 4
POOL_GROUP = D_MODEL // N_POOL_GROUPS
HEAD_DIM = 128
N_HEADS = D_MODEL // HEAD_DIM
N_KV_HEADS = 4
KV_GROUP = N_HEADS // N_KV_HEADS
IDX_HEADS = 16
IDX_DIM = 128
INDEX_TOPK = 256
Q_BLOCK = 128
ROPE_THETA = 500000.0
ROT_DIM = HEAD_DIM // 4
Q_COLS = N_HEADS * HEAD_DIM
K_COLS = N_KV_HEADS * HEAD_DIM
V_COLS = N_KV_HEADS * HEAD_DIM
IQ_COLS = IDX_HEADS * IDX_DIM
IK_COLS = IDX_DIM
IW_COLS = IDX_HEADS
DSA_IN = Q_COLS + K_COLS + V_COLS + IQ_COLS + IK_COLS + IW_COLS
N_MEM = 256
XATTN_HEADS = 4
XATTN_DIM = 128
XATTN_WIDTH = XATTN_HEADS * XATTN_DIM
D_FF = ((8 * D_MODEL + 3 * 256 - 1) // (3 * 256)) * 256
NORM_EPS = 1e-6

kernel_name = "hybrid_pool_dsa_memxattn_block"


def rms_norm(x, g):
    xf = x.astype(jnp.float32)
    y = xf * lax.rsqrt(jnp.mean(xf * xf, axis=-1, keepdims=True) + NORM_EPS)
    return (y * g.astype(jnp.float32)).astype(x.dtype)


def rope_tables(positions):
    inv = ROPE_THETA ** (-jnp.arange(0, ROT_DIM, 2, dtype=jnp.float32) / ROT_DIM)
    ang = positions.astype(jnp.float32)[..., None] * inv
    return jnp.cos(ang), jnp.sin(ang)


def apply_partial_rope(x, cos, sin):
    half = ROT_DIM // 2
    xr = x[..., :ROT_DIM].astype(jnp.float32)
    x1, x2 = xr[..., :half], xr[..., half:]
    c = cos[:, :, None, :]
    s = sin[:, :, None, :]
    rot = jnp.concatenate([x1 * c - x2 * s, x2 * c + x1 * s], axis=-1).astype(x.dtype)
    return jnp.concatenate([rot, x[..., ROT_DIM:]], axis=-1)


def pool_mixer(xn, w_group, scale):
    B, S, D = xn.shape
    xf = xn.astype(jnp.float32).reshape(B, S, N_POOL_GROUPS, POOL_GROUP)
    cs = jnp.cumsum(xf, axis=1)
    t = jnp.arange(S)
    outs = []
    for g, win in enumerate(POOL_WINDOWS):
        c = cs[:, :, g]
        lag = jnp.pad(c[:, :S - win], ((0, 0), (win, 0), (0, 0)))
        cnt = jnp.minimum(t + 1, win).astype(jnp.float32)[None, :, None]
        outs.append((c - lag) / cnt - xf[:, :, g])
    p = jnp.stack(outs, axis=2).astype(xn.dtype)
    y = jnp.einsum('bsgc,gcd->bsgd', p, w_group).reshape(B, S, D)
    return y * scale


def dsa_mixer(xn, w_in, w_out, cos, sin):
    B, S, D = xn.shape
    proj = xn @ w_in
    splits = [Q_COLS, Q_COLS + K_COLS, Q_COLS + K_COLS + V_COLS,
              Q_COLS + K_COLS + V_COLS + IQ_COLS,
              Q_COLS + K_COLS + V_COLS + IQ_COLS + IK_COLS]
    q, k, v, iq, ik, iw = jnp.split(proj, splits, axis=-1)
    q = apply_partial_rope(q.reshape(B, S, N_HEADS, HEAD_DIM), cos, sin)
    k = apply_partial_rope(k.reshape(B, S, N_KV_HEADS, HEAD_DIM), cos, sin)
    v = v.reshape(B, S, N_KV_HEADS, HEAD_DIM)
    iq = apply_partial_rope(iq.reshape(B, S, IDX_HEADS, IDX_DIM), cos, sin)
    ik = apply_partial_rope(ik.reshape(B, S, 1, IDX_DIM), cos, sin)[:, :, 0]
    iw = iw.astype(jnp.float32) * (IDX_HEADS ** -0.5)
    topk = min(INDEX_TOPK, S // 4)
    n_blocks = S // Q_BLOCK
    key_pos = jnp.arange(S)

    def block(i):
        start = i * Q_BLOCK
        qb = lax.dynamic_slice_in_dim(q, start, Q_BLOCK, axis=1)
        iqb = lax.dynamic_slice_in_dim(iq, start, Q_BLOCK, axis=1)
        iwb = lax.dynamic_slice_in_dim(iw, start, Q_BLOCK, axis=1)
        qpos = start + jnp.arange(Q_BLOCK)
        logits = jnp.einsum('bqhd,bsd->bqhs', iqb, ik).astype(jnp.float32) * (IDX_DIM ** -0.5)
        score = jnp.einsum('bqhs,bqh->bqs', jax.nn.relu(logits), iwb)
        causal = key_pos[None, :] <= qpos[:, None]
        score = jnp.where(causal[None], score, -jnp.inf)
        _, idx = lax.top_k(score, topk)
        valid = idx <= qpos[None, :, None]
        kg = jax.vmap(lambda kb, ib: kb[ib])(k, idx)
        vg = jax.vmap(lambda vb, ib: vb[ib])(v, idx)
        qg = qb.reshape(B, Q_BLOCK, N_KV_HEADS, KV_GROUP, HEAD_DIM)
        s = jnp.einsum('bqhgd,bqnhd->bqhgn', qg, kg).astype(jnp.float32) * (HEAD_DIM ** -0.5)
        s = jnp.where(valid[:, :, None, None, :], s, -jnp.inf)
        p = jax.nn.softmax(s, axis=-1).astype(vg.dtype)
        o = jnp.einsum('bqhgn,bqnhd->bqhgd', p, vg)
        return o.reshape(B, Q_BLOCK, Q_COLS)

    o = lax.map(block, jnp.arange(n_blocks))
    o = jnp.moveaxis(o, 0, 1).reshape(B, S, Q_COLS)
    return o @ w_out


def memory_xattn(hn, mem_n, wq, wkv, wo):
    B, S, _ = hn.shape
    M = mem_n.shape[1]
    q = (hn @ wq).reshape(B, S, XATTN_HEADS, XATTN_DIM)
    km, vm = jnp.split(mem_n @ wkv, 2, axis=-1)
    km = km.reshape(B, M, XATTN_HEADS, XATTN_DIM)
    vm = vm.reshape(B, M, XATTN_HEADS, XATTN_DIM)
    s = jnp.einsum('bshd,bmhd->bhsm', q, km).astype(jnp.float32) * (XATTN_DIM ** -0.5)
    p = jax.nn.softmax(s, axis=-1).astype(vm.dtype)
    o = jnp.einsum('bhsm,bmhd->bshd', p, vm).reshape(B, S, XATTN_WIDTH)
    return o @ wo


def swiglu(hn, w_gate_up, w_down):
    g, u = jnp.split(hn @ w_gate_up, 2, axis=-1)
    return (jax.nn.silu(g) * u) @ w_down


def setup_inputs(seed: int = 0) -> dict:
    key = jax.random.key(seed)
    ks = jax.random.split(key, 16)
    f32 = jnp.float32
    D = D_MODEL
    x = jax.random.normal(ks[0], (BATCH, SEQ, D), f32)
    mem = jax.random.normal(ks[1], (BATCH, N_MEM, D), f32)
    offset = jax.random.randint(ks[2], (BATCH, 1), 0, 4096, dtype=jnp.int32)
    positions = (offset + jnp.arange(SEQ, dtype=jnp.int32)[None, :]).astype(jnp.int32)
    norm_gains = 1.0 + 0.02 * jax.random.normal(ks[3], (DEPTH, 6, D), f32)
    mem_norm = 1.0 + 0.02 * jax.random.normal(ks[4], (D,), f32)
    pool_w = jax.random.normal(ks[5], (N_POOL_LAYERS, N_POOL_GROUPS, POOL_GROUP, POOL_GROUP), f32) * POOL_GROUP ** -0.5
    pool_scale = 1.0 + 0.02 * jax.random.normal(ks[6], (N_POOL_LAYERS, D), f32)
    dsa_w_in = jax.random.normal(ks[7], (N_DSA_LAYERS, D, DSA_IN), f32) * D ** -0.5
    dsa_w_out = jax.random.normal(ks[8], (N_DSA_LAYERS, Q_COLS, D), f32) * Q_COLS ** -0.5
    xattn_wq = jax.random.normal(ks[9], (DEPTH, D, XATTN_WIDTH), f32) * D ** -0.5
    xattn_wkv = jax.random.normal(ks[10], (DEPTH, D, 2 * XATTN_WIDTH), f32) * D ** -0.5
    xattn_wo = jax.random.normal(ks[11], (DEPTH, XATTN_WIDTH, D), f32) * XATTN_WIDTH ** -0.5
    ffn_w_gate_up = jax.random.normal(ks[12], (DEPTH, D, 2 * D_FF), f32) * D ** -0.5
    ffn_w_down = jax.random.normal(ks[13], (DEPTH, D_FF, D), f32) * D_FF ** -0.5
    return {"x": x, "mem": mem, "positions": positions, "norm_gains": norm_gains,
            "mem_norm": mem_norm, "pool_w": pool_w, "pool_scale": pool_scale,
            "dsa_w_in": dsa_w_in, "dsa_w_out": dsa_w_out, "xattn_wq": xattn_wq,
            "xattn_wkv": xattn_wkv, "xattn_wo": xattn_wo,
            "ffn_w_gate_up": ffn_w_gate_up, "ffn_w_down": ffn_w_down}


def reference(x, mem, positions, norm_gains, mem_norm, pool_w, pool_scale, dsa_w_in,
              dsa_w_out, xattn_wq, xattn_wkv, xattn_wo, ffn_w_gate_up, ffn_w_down):
    cos, sin = rope_tables(positions)
    mem_n = rms_norm(mem, mem_norm)
    h = x
    for i in range(DEPTH):
        g = norm_gains[i]
        a = rms_norm(h, g[0])
        if i % N_MIXERS == 0:
            a = pool_mixer(a, pool_w[i // N_MIXERS], pool_scale[i // N_MIXERS])
        else:
            a = dsa_mixer(a, dsa_w_in[i // N_MIXERS], dsa_w_out[i // N_MIXERS], cos, sin)
        h = h + rms_norm(a, g[1])
        c = memory_xattn(rms_norm(h, g[2]), mem_n, xattn_wq[i], xattn_wkv[i], xattn_wo[i])
        h = h + rms_norm(c, g[3])
        f = swiglu(rms_norm(h, g[4]), ffn_w_gate_up[i], ffn_w_down[i])
        h = h + rms_norm(f, g[5])
    return h
```

```python
import functools

import jax
import jax.numpy as jnp
from jax import lax
from jax.experimental import pallas as pl
from jax.experimental.pallas import tpu as pltpu

F32 = jnp.float32
BF16 = jnp.bfloat16
I32 = jnp.int32

LANES = 128
SUBLANES = 8
VMEM_LIMIT_BYTES = 56 * 1024 * 1024

NORM_EPS = 1e-6
POOL_WINDOWS = (2, 4, 8, 16)
MAX_WINDOW = 16
HEAD_DIM = 128
N_KV_HEADS = 4
KV_GROUP = 4
IDX_HEADS = 16
IDX_DIM = 128
INDEX_TOPK = 256
ROPE_THETA = 500000.0
ROT_DIM = 32
XATTN_HEADS = 4
XATTN_DIM = 128
INT_MIN = -(2 ** 31)


def _params(semantics):
    return pltpu.CompilerParams(dimension_semantics=semantics, vmem_limit_bytes=VMEM_LIMIT_BYTES)


def _tile(n, want):
    t = min(n, want)
    while n % t:
        t //= 2
    return t


def _norm_body(x_ref, g_ref):
    x = x_ref[...].astype(F32)
    ms = jnp.mean(x * x, axis=-1, keepdims=True)
    return x * lax.rsqrt(ms + NORM_EPS) * g_ref[...]


def _norm_kernel(x_ref, g_ref, o_ref):
    o_ref[...] = _norm_body(x_ref, g_ref).astype(o_ref.dtype)


def _norm_res_kernel(x_ref, g_ref, r_ref, o_ref):
    o_ref[...] = (r_ref[...] + _norm_body(x_ref, g_ref)).astype(o_ref.dtype)


def rms_norm(x, g, res=None, out_dtype=F32):
    n, d = x.shape
    tr = _tile(n, 256)
    row = pl.BlockSpec((tr, d), lambda i: (i, 0))
    gain = pl.BlockSpec((1, d), lambda i: (0, 0))
    g2 = g.reshape(1, d).astype(F32)
    if res is None:
        kern, specs, args = _norm_kernel, [row, gain], (x, g2)
    else:
        kern, specs, args = _norm_res_kernel, [row, gain, row], (x, g2, res)
    return pl.pallas_call(
        kern, name="rms_norm", grid=(n // tr,), in_specs=specs, out_specs=row,
        out_shape=jax.ShapeDtypeStruct((n, d), out_dtype),
        compiler_params=_params(("parallel",)))(*args)


def _mm_kernel(a_ref, b_ref, o_ref, acc_ref, *, nk):
    k = pl.program_id(2)

    @pl.when(k == 0)
    def _():
        acc_ref[...] = jnp.zeros_like(acc_ref)

    acc_ref[...] += jnp.dot(a_ref[...].astype(BF16), b_ref[...].astype(BF16),
                            preferred_element_type=F32)

    @pl.when(k == nk - 1)
    def _():
        o_ref[...] = acc_ref[...].astype(o_ref.dtype)


def matmul(a, b, out_dtype=F32, tm=512, tn=512, tk=1024):
    m, kd = a.shape
    n = b.shape[1]
    tm, tn, tk = _tile(m, tm), _tile(n, tn), _tile(kd, tk)
    nk = kd // tk
    return pl.pallas_call(
        functools.partial(_mm_kernel, nk=nk), name="matmul",
        grid=(m // tm, n // tn, nk),
        in_specs=[pl.BlockSpec((tm, tk), lambda i, j, k: (i, k)),
                  pl.BlockSpec((tk, tn), lambda i, j, k: (k, j))],
        out_specs=pl.BlockSpec((tm, tn), lambda i, j, k: (i, j)),
        out_shape=jax.ShapeDtypeStruct((m, n), out_dtype),
        scratch_shapes=[pltpu.VMEM((tm, tn), F32)],
        compiler_params=_params(("parallel", "parallel", "arbitrary")))(a, b)


def _swiglu_kernel(a_ref, bg_ref, bu_ref, o_ref):
    a = a_ref[...].astype(BF16)
    g = jnp.dot(a, bg_ref[...].astype(BF16), preferred_element_type=F32)
    u = jnp.dot(a, bu_ref[...].astype(BF16), preferred_element_type=F32)
    o_ref[...] = (g * (1.0 / (1.0 + jnp.exp(-g))) * u).astype(o_ref.dtype)


def swiglu_up(a, w_gate_up, tm=512, tn=512):
    m, kd = a.shape
    f = w_gate_up.shape[1] // 2
    tm, tn = _tile(m, tm), _tile(f, tn)
    nf = f // tn
    return pl.pallas_call(
        _swiglu_kernel, name="swiglu_up",
        grid=(m // tm, nf),
        in_specs=[pl.BlockSpec((tm, kd), lambda i, j: (i, 0)),
                  pl.BlockSpec((kd, tn), lambda i, j: (0, j)),
                  pl.BlockSpec((kd, tn), lambda i, j: (0, j + nf))],
        out_specs=pl.BlockSpec((tm, tn), lambda i, j: (i, j)),
        out_shape=jax.ShapeDtypeStruct((m, f), BF16),
        compiler_params=_params(("parallel", "parallel")))(a, w_gate_up, w_gate_up)


def _pool_kernel(x_ref, halo_ref, w_ref, scale_ref, o_ref, pad_ref, *, ts, cg):
    i = pl.program_id(1)
    row = i * ts + lax.broadcasted_iota(I32, (ts, 1), 0)
    for g, win in enumerate(POOL_WINDOWS):
        cols = slice(g * cg, (g + 1) * cg)
        x = x_ref[0, :, cols]
        halo = halo_ref[0, :, cols]
        pad_ref[0:MAX_WINDOW, :] = jnp.where(i == 0, jnp.zeros_like(halo), halo)
        pad_ref[MAX_WINDOW:, :] = x
        wsum = x
        for k in range(1, win):
            wsum = wsum + pad_ref[MAX_WINDOW - k:MAX_WINDOW - k + ts, :]
        cnt = jnp.minimum(row + 1, win).astype(F32)
        p = wsum / cnt - x
        y = jnp.dot(p.astype(BF16), w_ref[g].astype(BF16), preferred_element_type=F32)
        o_ref[0, :, cols] = y * scale_ref[:, cols]


def pool_mixer(xn, w_group, scale):
    b, s, d = xn.shape
    g, cg, _ = w_group.shape
    ts = _tile(s, 512)
    hb = ts // MAX_WINDOW
    return pl.pallas_call(
        functools.partial(_pool_kernel, ts=ts, cg=cg), name="pool_mixer",
        grid=(b, s // ts),
        in_specs=[pl.BlockSpec((1, ts, d), lambda bi, i: (bi, i, 0)),
                  pl.BlockSpec((1, MAX_WINDOW, d), lambda bi, i: (bi, jnp.maximum(i * hb - 1, 0), 0)),
                  pl.BlockSpec((g, cg, cg), lambda bi, i: (0, 0, 0)),
                  pl.BlockSpec((1, d), lambda bi, i: (0, 0))],
        out_specs=pl.BlockSpec((1, ts, d), lambda bi, i: (bi, i, 0)),
        out_shape=jax.ShapeDtypeStruct((b, s, d), F32),
        scratch_shapes=[pltpu.VMEM((MAX_WINDOW + ts, cg), F32)],
        compiler_params=_params(("parallel", "arbitrary")))(xn, xn, w_group.astype(BF16), scale.reshape(1, d))


def _xattn_kernel(q_ref, k_ref, v_ref, o_ref):
    scale = XATTN_DIM ** -0.5
    for h in range(XATTN_HEADS):
        cols = slice(h * XATTN_DIM, (h + 1) * XATTN_DIM)
        q = q_ref[0, :, cols].astype(BF16)
        k = k_ref[0, :, cols].astype(BF16)
        v = v_ref[0, :, cols].astype(BF16)
        s = lax.dot_general(q, k, (((1,), (1,)), ((), ())), preferred_element_type=F32) * scale
        e = jnp.exp(s - jnp.max(s, axis=-1, keepdims=True))
        p = e / jnp.sum(e, axis=-1, keepdims=True)
        o_ref[0, :, cols] = jnp.dot(p.astype(BF16), v, preferred_element_type=F32).astype(o_ref.dtype)


def xattn_core(q, km, vm):
    b, s, w = q.shape
    m = km.shape[1]
    ts = _tile(s, 512)
    return pl.pallas_call(
        _xattn_kernel, name="xattn_core",
        grid=(b, s // ts),
        in_specs=[pl.BlockSpec((1, ts, w), lambda bi, i: (bi, i, 0)),
                  pl.BlockSpec((1, m, w), lambda bi, i: (bi, 0, 0)),
                  pl.BlockSpec((1, m, w), lambda bi, i: (bi, 0, 0))],
        out_specs=pl.BlockSpec((1, ts, w), lambda bi, i: (bi, i, 0)),
        out_shape=jax.ShapeDtypeStruct((b, s, w), BF16),
        compiler_params=_params(("parallel", "parallel")))(q, km, vm)


def _rope_kernel(x_ref, pos_ref, inv_ref, o_ref, *, head_major):
    x = x_ref[0].astype(F32)
    ang = pos_ref[0].astype(F32) * inv_ref[...]
    lane = lax.broadcasted_iota(I32, x.shape, 1)
    half = ROT_DIM // 2
    c = jnp.where(lane < ROT_DIM, jnp.cos(ang), 1.0)
    sn = jnp.sin(ang)
    sgn = jnp.where(lane < half, -sn, jnp.where(lane < ROT_DIM, sn, 0.0))
    partner = jnp.where(lane < half, pltpu.roll(x, LANES - half, 1), pltpu.roll(x, half, 1))
    y = (x * c + partner * sgn).astype(o_ref.dtype)
    if head_major:
        o_ref[0, 0, 0] = y
    else:
        o_ref[0] = y


def rope(x, pos, inv_lane, n_heads, out_dtype, head_major_tile=None):
    b, s, _ = x.shape
    ts = head_major_tile or _tile(s, 512)
    if head_major_tile:
        out_shape = jax.ShapeDtypeStruct((b, s // ts, n_heads, ts, HEAD_DIM), out_dtype)
        out_spec = pl.BlockSpec((1, 1, 1, ts, HEAD_DIM), lambda bi, i, h: (bi, i, h, 0, 0))
    else:
        out_shape = jax.ShapeDtypeStruct(x.shape, out_dtype)
        out_spec = pl.BlockSpec((1, ts, HEAD_DIM), lambda bi, i, h: (bi, i, h))
    return pl.pallas_call(
        functools.partial(_rope_kernel, head_major=bool(head_major_tile)), name="rope",
        grid=(b, s // ts, n_heads),
        in_specs=[pl.BlockSpec((1, ts, HEAD_DIM), lambda bi, i, h: (bi, i, h)),
                  pl.BlockSpec((1, ts, 1), lambda bi, i, h: (bi, i, 0)),
                  pl.BlockSpec((1, HEAD_DIM), lambda bi, i, h: (0, 0))],
        out_specs=out_spec, out_shape=out_shape,
        compiler_params=_params(("parallel", "parallel", "parallel")))(x, pos, inv_lane)


def _index_kernel(ik_ref, iq_ref, w_ref, o_ref, x_ref, *, tq, tk, rc):
    qi = pl.program_id(1)
    kj = pl.program_id(2)
    needed = kj * tk <= qi * tq + tq - 1

    @pl.when(needed)
    def _():
        x_ref[...] = lax.dot_general(ik_ref[0], iq_ref[0, 0], (((1,), (1,)), ((), ())),
                                     preferred_element_type=F32)
        w = w_ref[0, 0] * (IDX_HEADS ** -0.5 * IDX_DIM ** -0.5)

        def chunk(r, carry):
            rows = pl.ds(pl.multiple_of(r * rc, rc), rc)
            acc = jnp.zeros((rc, tq), F32)
            for h in range(IDX_HEADS):
                acc = acc + jnp.maximum(x_ref[rows, h * tq:(h + 1) * tq], 0.0) * w[h:h + 1, :]
            o_ref[0, rows, :] = acc
            return carry

        lax.fori_loop(0, tk // rc, chunk, 0)

    @pl.when(jnp.logical_not(needed))
    def _():
        o_ref[...] = jnp.zeros_like(o_ref)


def index_scores(ik, iq_hm, w_hm, tq, tk):
    b, s, _ = ik.shape
    nq, nk = s // tq, s // tk

    def last_needed(qi):
        return (qi * tq + tq - 1) // tk

    return pl.pallas_call(
        functools.partial(_index_kernel, tq=tq, tk=tk, rc=min(tk, 32)), name="index_scores",
        grid=(b, nq, nk),
        in_specs=[pl.BlockSpec((1, tk, IDX_DIM), lambda bi, qi, kj: (bi, jnp.minimum(kj, last_needed(qi)), 0)),
                  pl.BlockSpec((1, 1, IDX_HEADS * tq, IDX_DIM), lambda bi, qi, kj: (bi, qi, 0, 0)),
                  pl.BlockSpec((1, 1, IDX_HEADS, tq), lambda bi, qi, kj: (bi, qi, 0, 0))],
        out_specs=pl.BlockSpec((1, tk, tq), lambda bi, qi, kj: (bi, kj, qi)),
        out_shape=jax.ShapeDtypeStruct((b, s, s), F32),
        scratch_shapes=[pltpu.VMEM((tk, IDX_HEADS * tq), F32)],
        compiler_params=_params(("parallel", "parallel", "arbitrary")))(ik, iq_hm, w_hm)


def _select_kernel(s_ref, idx_ref, key_ref, c_ref, *, tl, rb, topk):
    qi = pl.program_id(1)
    t_lane = qi * tl + lax.broadcasted_iota(I32, (1, tl), 1)
    nblk = ((qi + 1) * tl + rb - 1) // rb
    sub = rb // SUBLANES

    def rows_of(r):
        return pl.ds(pl.multiple_of(r * rb, rb), rb)

    def row_ids(r):
        return r * rb + lax.broadcasted_iota(I32, (rb, tl), 0)

    def fold(m):
        return jnp.sum(m.reshape(sub, SUBLANES, tl), axis=0)

    def make_keys(r, carry):
        x = s_ref[0, rows_of(r), :]
        x = jnp.where(x == 0.0, 0.0, x)
        bits = lax.bitcast_convert_type(x, I32)
        key = bits ^ ((bits >> 31) & 0x7FFFFFFF)
        key_ref[rows_of(r), :] = jnp.where(row_ids(r) <= t_lane, key, INT_MIN)
        return carry

    lax.fori_loop(0, nblk, make_keys, 0)

    def count_ge(cand):
        def body(r, acc):
            return acc + fold((key_ref[rows_of(r), :] >= cand).astype(I32))
        acc = lax.fori_loop(0, nblk, body, jnp.zeros((SUBLANES, tl), I32))
        return jnp.sum(acc, axis=0, keepdims=True)

    zero = jnp.zeros((1, tl), I32)
    thr = jnp.where(count_ge(zero) >= topk, zero, jnp.full((1, tl), INT_MIN, I32))

    def bit_step(i, thr):
        cand = thr | (jnp.int32(1) << (30 - i))
        return jnp.where(count_ge(cand) >= topk, cand, thr)

    thr = lax.fori_loop(0, 31, bit_step, thr)

    def count_gt(r, acc):
        return acc + fold((key_ref[rows_of(r), :] > thr).astype(I32))

    n_gt = jnp.sum(lax.fori_loop(0, nblk, count_gt, jnp.zeros((SUBLANES, tl), I32)), axis=0, keepdims=True)
    need = (topk - n_gt).astype(F32)

    ri = lax.broadcasted_iota(I32, (rb, rb), 0)
    ci = lax.broadcasted_iota(I32, (rb, rb), 1)
    tri = (ci <= ri).astype(BF16)

    def prefix(r, carry):
        carry_eq, carry_c = carry
        key = key_ref[rows_of(r), :]
        eq = jnp.logical_and(key == thr, row_ids(r) <= t_lane)
        ceq = jnp.dot(tri, eq.astype(BF16), preferred_element_type=F32) + carry_eq
        sel = jnp.logical_or(key > thr, jnp.logical_and(eq, ceq <= need))
        c = jnp.dot(tri, sel.astype(BF16), preferred_element_type=F32) + carry_c
        c_ref[rows_of(r), :] = c
        return ceq[rb - 1:rb, :], c[rb - 1:rb, :]

    zf = jnp.zeros((1, tl), F32)
    lax.fori_loop(0, nblk, prefix, (zf, zf))

    group = SUBLANES

    def slots(jb, carry):
        def body(r, accs):
            c = c_ref[rows_of(r), :]
            return tuple(a + fold((c <= (jb * group + u).astype(F32)).astype(F32)) for u, a in enumerate(accs))
        accs = lax.fori_loop(0, nblk, body, tuple(jnp.zeros((SUBLANES, tl), F32) for _ in range(group)))
        out = jnp.concatenate([jnp.sum(a, axis=0, keepdims=True) for a in accs], axis=0)
        idx_ref[0, pl.ds(pl.multiple_of(jb * group, group), group), :] = out.astype(I32)
        return carry

    lax.fori_loop(0, topk // group, slots, 0)


def select_topk(s_t, topk):
    b, s, _ = s_t.shape
    tl = _tile(s, LANES)
    rb = _tile(s, 256)
    return pl.pallas_call(
        functools.partial(_select_kernel, tl=tl, rb=rb, topk=topk), name="select_topk",
        grid=(b, s // tl),
        in_specs=[pl.BlockSpec((1, s, tl), lambda bi, qi: (bi, 0, qi))],
        out_specs=pl.BlockSpec((1, topk, tl), lambda bi, qi: (bi, 0, qi)),
        out_shape=jax.ShapeDtypeStruct((b, topk, s), I32),
        scratch_shapes=[pltpu.VMEM((s, tl), I32), pltpu.VMEM((s, tl), F32)],
        compiler_params=_params(("parallel", "parallel")))(s_t)


def _sparse_attn_kernel(idx_ref, q_ref, k_ref, v_ref, o_ref, kst_ref, vst_ref, *, tq, s_len, topk):
    qt = pl.program_id(2)
    scale = HEAD_DIM ** -0.5
    slot = lax.broadcasted_iota(I32, (KV_GROUP, topk), 1)

    def per_query(qq, carry):
        def gather(j, c):
            r = jnp.minimum(idx_ref[0, qq, j], s_len - 1)
            kst_ref[pl.ds(j, 1), :] = k_ref[0, pl.ds(r, 1), :]
            vst_ref[pl.ds(j, 1), :] = v_ref[0, pl.ds(r, 1), :]
            return c

        lax.fori_loop(0, topk, gather, 0, unroll=8)
        q = q_ref[0, qq, 0]
        s = lax.dot_general(q, kst_ref[...].astype(BF16), (((1,), (1,)), ((), ())),
                            preferred_element_type=F32) * scale
        n_valid = jnp.minimum(qt * tq + qq + 1, topk)
        s = jnp.where(slot < n_valid, s, -jnp.inf)
        e = jnp.exp(s - jnp.max(s, axis=-1, keepdims=True))
        p = e / jnp.sum(e, axis=-1, keepdims=True)
        o_ref[0, qq, 0] = jnp.dot(p.astype(BF16), vst_ref[...].astype(BF16), preferred_element_type=F32)
        return carry

    lax.fori_loop(0, tq, per_query, 0)


def sparse_attention(idx, q, k, v, topk):
    b, s, _ = idx.shape
    tq = _tile(s, 16)
    return pl.pallas_call(
        functools.partial(_sparse_attn_kernel, tq=tq, s_len=s, topk=topk), name="sparse_attention",
        grid=(b, N_KV_HEADS, s // tq),
        in_specs=[pl.BlockSpec((1, tq, topk), lambda bi, h, qt: (bi, qt, 0), memory_space=pltpu.SMEM),
                  pl.BlockSpec((1, tq, 1, KV_GROUP, HEAD_DIM), lambda bi, h, qt: (bi, qt, h, 0, 0)),
                  pl.BlockSpec((1, s, HEAD_DIM), lambda bi, h, qt: (bi, 0, h)),
                  pl.BlockSpec((1, s, HEAD_DIM), lambda bi, h, qt: (bi, 0, h))],
        out_specs=pl.BlockSpec((1, tq, 1, KV_GROUP, HEAD_DIM), lambda bi, h, qt: (bi, qt, h, 0, 0)),
        out_shape=jax.ShapeDtypeStruct((b, s, N_KV_HEADS, KV_GROUP, HEAD_DIM), F32),
        scratch_shapes=[pltpu.VMEM((topk, HEAD_DIM), F32), pltpu.VMEM((topk, HEAD_DIM), F32)],
        compiler_params=_params(("parallel", "parallel", "arbitrary")))(idx, q, k, v)


def dsa_mixer(xn, w_in, w_out, positions):
    b, s, d = xn.shape
    n = b * s
    q_cols = d
    kv_cols = N_KV_HEADS * HEAD_DIM
    iq_cols = IDX_HEADS * IDX_DIM
    o0 = 0
    bounds = []
    for width in (q_cols, kv_cols, kv_cols, iq_cols, IDX_DIM, IDX_HEADS):
        bounds.append((o0, o0 + width))
        o0 += width
    w_bf = w_in.astype(BF16)
    x2 = xn.reshape(n, d)
    q, k, v, iq, ik, iw = (matmul(x2, w_bf[:, lo:hi]).reshape(b, s, hi - lo) for lo, hi in bounds)

    inv = ROPE_THETA ** (-jnp.arange(0, ROT_DIM, 2, dtype=F32) / ROT_DIM)
    inv_lane = jnp.tile(inv, LANES // inv.shape[0]).reshape(1, LANES)
    pos = positions.reshape(b, s, 1)
    tq = _tile(s, 256)
    tk = _tile(s, 512)
    q_r = rope(q, pos, inv_lane, d // HEAD_DIM, BF16)
    k_r = rope(k, pos, inv_lane, N_KV_HEADS, F32)
    iq_hm = rope(iq, pos, inv_lane, IDX_HEADS, BF16, head_major_tile=tq)
    ik_r = rope(ik, pos, inv_lane, 1, BF16)
    w_hm = iw.reshape(b, s // tq, tq, IDX_HEADS).transpose(0, 1, 3, 2)

    topk = min(INDEX_TOPK, s // 4)
    s_t = index_scores(ik_r, iq_hm.reshape(b, s // tq, IDX_HEADS * tq, IDX_DIM), w_hm, tq, tk)
    idx = select_topk(s_t, topk).transpose(0, 2, 1)
    o = sparse_attention(idx, q_r.reshape(b, s, N_KV_HEADS, KV_GROUP, HEAD_DIM), k_r, v, topk)
    return matmul(o.reshape(n, d), w_out.astype(BF16))


def memory_xattn(hn, mem_n, wq, wkv, wo, b, s):
    width = wq.shape[1]
    m = mem_n.shape[0] // b
    q = matmul(hn, wq.astype(BF16), out_dtype=BF16).reshape(b, s, width)
    kv = matmul(mem_n, wkv.astype(BF16), out_dtype=BF16)
    km = kv[:, :width].reshape(b, m, width)
    vm = kv[:, width:].reshape(b, m, width)
    o = xattn_core(q, km, vm)
    return matmul(o.reshape(b * s, width), wo.astype(BF16))


def kernel(x, mem, positions, norm_gains, mem_norm, pool_w, pool_scale, dsa_w_in, dsa_w_out,
           xattn_wq, xattn_wkv, xattn_wo, ffn_w_gate_up, ffn_w_down):
    b, s, d = x.shape
    n = b * s
    depth = norm_gains.shape[0]
    mem_n = rms_norm(mem.reshape(-1, d), mem_norm, out_dtype=BF16)
    h = x.reshape(n, d)
    for i in range(depth):
        g = norm_gains[i]
        if i % 2 == 0:
            a = rms_norm(h, g[0])
            a = pool_mixer(a.reshape(b, s, d), pool_w[i // 2], pool_scale[i // 2]).reshape(n, d)
        else:
            a = rms_norm(h, g[0], out_dtype=BF16)
            a = dsa_mixer(a.reshape(b, s, d), dsa_w_in[i // 2], dsa_w_out[i // 2], positions)
        h = rms_norm(a, g[1], res=h)
        c = memory_xattn(rms_norm(h, g[2], out_dtype=BF16), mem_n, xattn_wq[i], xattn_wkv[i], xattn_wo[i], b, s)
        h = rms_norm(c, g[3], res=h)
        act = swiglu_up(rms_norm(h, g[4], out_dtype=BF16), ffn_w_gate_up[i].astype(BF16))
        f = matmul(act, ffn_w_down[i].astype(BF16))
        h = rms_norm(f, g[5], res=h)
    return h.reshape(b, s, d)
```

```python
import functools

import jax
import jax.numpy as jnp
from jax import lax
from jax.experimental import pallas as pl
from jax.experimental.pallas import tpu as pltpu

F32 = jnp.float32
BF16 = jnp.bfloat16
I32 = jnp.int32

LANES = 128
SUBLANES = 8
VMEM_LIMIT_BYTES = 56 * 1024 * 1024

NORM_EPS = 1e-6
POOL_WINDOWS = (2, 4, 8, 16)
MAX_WINDOW = 16
HEAD_DIM = 128
N_KV_HEADS = 4
KV_GROUP = 4
IDX_HEADS = 16
IDX_DIM = 128
INDEX_TOPK = 256
ROPE_THETA = 500000.0
ROT_DIM = 32
XATTN_HEADS = 4
XATTN_DIM = 128
INT_MIN = -(2 ** 31)


def _params(semantics):
    return pltpu.CompilerParams(dimension_semantics=semantics, vmem_limit_bytes=VMEM_LIMIT_BYTES)


def _tile(n, want):
    t = min(n, want)
    while n % t:
        t //= 2
    return t


def _norm_body(x_ref, g_ref):
    x = x_ref[...].astype(F32)
    ms = jnp.mean(x * x, axis=-1, keepdims=True)
    return x * lax.rsqrt(ms + NORM_EPS) * g_ref[...]


def _norm_kernel(x_ref, g_ref, o_ref):
    o_ref[...] = _norm_body(x_ref, g_ref).astype(o_ref.dtype)


def _norm_res_kernel(x_ref, g_ref, r_ref, o_ref):
    o_ref[...] = (r_ref[...] + _norm_body(x_ref, g_ref)).astype(o_ref.dtype)


def rms_norm(x, g, res=None, out_dtype=F32):
    n, d = x.shape
    tr = _tile(n, 256)
    row = pl.BlockSpec((tr, d), lambda i: (i, 0))
    gain = pl.BlockSpec((1, d), lambda i: (0, 0))
    g2 = g.reshape(1, d).astype(F32)
    if res is None:
        kern, specs, args = _norm_kernel, [row, gain], (x, g2)
    else:
        kern, specs, args = _norm_res_kernel, [row, gain, row], (x, g2, res)
    return pl.pallas_call(
        kern, name="rms_norm", grid=(n // tr,), in_specs=specs, out_specs=row,
        out_shape=jax.ShapeDtypeStruct((n, d), out_dtype),
        compiler_params=_params(("parallel",)))(*args)


def _mm_kernel(a_ref, b_ref, o_ref, acc_ref, *, nk):
    k = pl.program_id(2)

    @pl.when(k == 0)
    def _():
        acc_ref[...] = jnp.zeros_like(acc_ref)

    acc_ref[...] += jnp.dot(a_ref[...].astype(BF16), b_ref[...].astype(BF16),
                            preferred_element_type=F32)

    @pl.when(k == nk - 1)
    def _():
        o_ref[...] = acc_ref[...].astype(o_ref.dtype)


def matmul(a, b, out_dtype=F32, tm=512, tn=512, tk=1024):
    m, kd = a.shape
    n = b.shape[1]
    tm, tn, tk = _tile(m, tm), _tile(n, tn), _tile(kd, tk)
    nk = kd // tk
    return pl.pallas_call(
        functools.partial(_mm_kernel, nk=nk), name="matmul",
        grid=(m // tm, n // tn, nk),
        in_specs=[pl.BlockSpec((tm, tk), lambda i, j, k: (i, k)),
                  pl.BlockSpec((tk, tn), lambda i, j, k: (k, j))],
        out_specs=pl.BlockSpec((tm, tn), lambda i, j, k: (i, j)),
        out_shape=jax.ShapeDtypeStruct((m, n), out_dtype),
        scratch_shapes=[pltpu.VMEM((tm, tn), F32)],
        compiler_params=_params(("parallel", "parallel", "arbitrary")))(a, b)


def _swiglu_kernel(a_ref, bg_ref, bu_ref, o_ref):
    a = a_ref[...].astype(BF16)
    g = jnp.dot(a, bg_ref[...].astype(BF16), preferred_element_type=F32)
    u = jnp.dot(a, bu_ref[...].astype(BF16), preferred_element_type=F32)
    o_ref[...] = (g * (1.0 / (1.0 + jnp.exp(-g))) * u).astype(o_ref.dtype)


def swiglu_up(a, w_gate_up, tm=512, tn=512):
    m, kd = a.shape
    f = w_gate_up.shape[1] // 2
    tm, tn = _tile(m, tm), _tile(f, tn)
    nf = f // tn
    return pl.pallas_call(
        _swiglu_kernel, name="swiglu_up",
        grid=(m // tm, nf),
        in_specs=[pl.BlockSpec((tm, kd), lambda i, j: (i, 0)),
                  pl.BlockSpec((kd, tn), lambda i, j: (0, j)),
                  pl.BlockSpec((kd, tn), lambda i, j: (0, j + nf))],
        out_specs=pl.BlockSpec((tm, tn), lambda i, j: (i, j)),
        out_shape=jax.ShapeDtypeStruct((m, f), BF16),
        compiler_params=_params(("parallel", "parallel")))(a, w_gate_up, w_gate_up)


def _pool_kernel(x_ref, halo_ref, w_ref, scale_ref, o_ref, pad_ref, *, ts, cg):
    i = pl.program_id(1)
    row = i * ts + lax.broadcasted_iota(I32, (ts, 1), 0)
    for g, win in enumerate(POOL_WINDOWS):
        cols = slice(g * cg, (g + 1) * cg)
        x = x_ref[0, :, cols]
        halo = halo_ref[0, :, cols]
        pad_ref[0:MAX_WINDOW, :] = jnp.where(i == 0, jnp.zeros_like(halo), halo)
        pad_ref[MAX_WINDOW:, :] = x
        wsum = x
        for k in range(1, win):
            wsum = wsum + pad_ref[MAX_WINDOW - k:MAX_WINDOW - k + ts, :]
        cnt = jnp.minimum(row + 1, win).astype(F32)
        p = wsum / cnt - x
        y = jnp.dot(p.astype(BF16), w_ref[g].astype(BF16), preferred_element_type=F32)
        o_ref[0, :, cols] = y * scale_ref[:, cols]


def pool_mixer(xn, w_group, scale):
    b, s, d = xn.shape
    g, cg, _ = w_group.shape
    ts = _tile(s, 512)
    hb = ts // MAX_WINDOW
    return pl.pallas_call(
        functools.partial(_pool_kernel, ts=ts, cg=cg), name="pool_mixer",
        grid=(b, s // ts),
        in_specs=[pl.BlockSpec((1, ts, d), lambda bi, i: (bi, i, 0)),
                  pl.BlockSpec((1, MAX_WINDOW, d), lambda bi, i: (bi, jnp.maximum(i * hb - 1, 0), 0)),
                  pl.BlockSpec((g, cg, cg), lambda bi, i: (0, 0, 0)),
                  pl.BlockSpec((1, d), lambda bi, i: (0, 0))],
        out_specs=pl.BlockSpec((1, ts, d), lambda bi, i: (bi, i, 0)),
        out_shape=jax.ShapeDtypeStruct((b, s, d), F32),
        scratch_shapes=[pltpu.VMEM((MAX_WINDOW + ts, cg), F32)],
        compiler_params=_params(("parallel", "arbitrary")))(xn, xn, w_group.astype(BF16), scale.reshape(1, d))


def _xattn_kernel(q_ref, k_ref, v_ref, o_ref):
    scale = XATTN_DIM ** -0.5
    for h in range(XATTN_HEADS):
        cols = slice(h * XATTN_DIM, (h + 1) * XATTN_DIM)
        q = q_ref[0, :, cols].astype(BF16)
        k = k_ref[0, :, cols].astype(BF16)
        v = v_ref[0, :, cols].astype(BF16)
        s = lax.dot_general(q, k, (((1,), (1,)), ((), ())), preferred_element_type=F32) * scale
        e = jnp.exp(s - jnp.max(s, axis=-1, keepdims=True))
        p = e / jnp.sum(e, axis=-1, keepdims=True)
        o_ref[0, :, cols] = jnp.dot(p.astype(BF16), v, preferred_element_type=F32).astype(o_ref.dtype)


def xattn_core(q, km, vm):
    b, s, w = q.shape
    m = km.shape[1]
    ts = _tile(s, 512)
    return pl.pallas_call(
        _xattn_kernel, name="xattn_core",
        grid=(b, s // ts),
        in_specs=[pl.BlockSpec((1, ts, w), lambda bi, i: (bi, i, 0)),
                  pl.BlockSpec((1, m, w), lambda bi, i: (bi, 0, 0)),
                  pl.BlockSpec((1, m, w), lambda bi, i: (bi, 0, 0))],
        out_specs=pl.BlockSpec((1, ts, w), lambda bi, i: (bi, i, 0)),
        out_shape=jax.ShapeDtypeStruct((b, s, w), BF16),
        compiler_params=_params(("parallel", "parallel")))(q, km, vm)


def _rope_kernel(x_ref, pos_ref, inv_ref, o_ref, *, head_major):
    x = x_ref[0].astype(F32)
    ang = pos_ref[0].astype(F32) * inv_ref[...]
    lane = lax.broadcasted_iota(I32, x.shape, 1)
    half = ROT_DIM // 2
    c = jnp.where(lane < ROT_DIM, jnp.cos(ang), 1.0)
    sn = jnp.sin(ang)
    sgn = jnp.where(lane < half, -sn, jnp.where(lane < ROT_DIM, sn, 0.0))
    partner = jnp.where(lane < half, pltpu.roll(x, LANES - half, 1), pltpu.roll(x, half, 1))
    y = (x * c + partner * sgn).astype(o_ref.dtype)
    if head_major:
        o_ref[0, 0, 0] = y
    else:
        o_ref[0] = y


def rope(x, pos, inv_lane, n_heads, out_dtype, head_major_tile=None):
    b, s, _ = x.shape
    ts = head_major_tile or _tile(s, 512)
    if head_major_tile:
        out_shape = jax.ShapeDtypeStruct((b, s // ts, n_heads, ts, HEAD_DIM), out_dtype)
        out_spec = pl.BlockSpec((1, 1, 1, ts, HEAD_DIM), lambda bi, i, h: (bi, i, h, 0, 0))
    else:
        out_shape = jax.ShapeDtypeStruct(x.shape, out_dtype)
        out_spec = pl.BlockSpec((1, ts, HEAD_DIM), lambda bi, i, h: (bi, i, h))
    return pl.pallas_call(
        functools.partial(_rope_kernel, head_major=bool(head_major_tile)), name="rope",
        grid=(b, s // ts, n_heads),
        in_specs=[pl.BlockSpec((1, ts, HEAD_DIM), lambda bi, i, h: (bi, i, h)),
                  pl.BlockSpec((1, ts, 1), lambda bi, i, h: (bi, i, 0)),
                  pl.BlockSpec((1, HEAD_DIM), lambda bi, i, h: (0, 0))],
        out_specs=out_spec, out_shape=out_shape,
        compiler_params=_params(("parallel", "parallel", "parallel")))(x, pos, inv_lane)


def _index_kernel(ik_ref, iq_ref, w_ref, o_ref, x_ref, *, tq, tk, rc):
    qi = pl.program_id(1)
    kj = pl.program_id(2)
    needed = kj * tk <= qi * tq + tq - 1

    @pl.when(needed)
    def _():
        x_ref[...] = lax.dot_general(ik_ref[0], iq_ref[0, 0], (((1,), (1,)), ((), ())),
                                     preferred_element_type=F32)
        w = w_ref[0, 0] * (IDX_HEADS ** -0.5 * IDX_DIM ** -0.5)

        def chunk(r, carry):
            rows = pl.ds(pl.multiple_of(r * rc, rc), rc)
            acc = jnp.zeros((rc, tq), F32)
            for h in range(IDX_HEADS):
                acc = acc + jnp.maximum(x_ref[rows, h * tq:(h + 1) * tq], 0.0) * w[h:h + 1, :]
            o_ref[0, rows, :] = acc
            return carry

        lax.fori_loop(0, tk // rc, chunk, 0)

    @pl.when(jnp.logical_not(needed))
    def _():
        o_ref[...] = jnp.zeros_like(o_ref)


def index_scores(ik, iq_hm, w_hm, tq, tk):
    b, s, _ = ik.shape
    nq, nk = s // tq, s // tk

    def last_needed(qi):
        return (qi * tq + tq - 1) // tk

    return pl.pallas_call(
        functools.partial(_index_kernel, tq=tq, tk=tk, rc=min(tk, 32)), name="index_scores",
        grid=(b, nq, nk),
        in_specs=[pl.BlockSpec((1, tk, IDX_DIM), lambda bi, qi, kj: (bi, jnp.minimum(kj, last_needed(qi)), 0)),
                  pl.BlockSpec((1, 1, IDX_HEADS * tq, IDX_DIM), lambda bi, qi, kj: (bi, qi, 0, 0)),
                  pl.BlockSpec((1, 1, IDX_HEADS, tq), lambda bi, qi, kj: (bi, qi, 0, 0))],
        out_specs=pl.BlockSpec((1, tk, tq), lambda bi, qi, kj: (bi, kj, qi)),
        out_shape=jax.ShapeDtypeStruct((b, s, s), F32),
        scratch_shapes=[pltpu.VMEM((tk, IDX_HEADS * tq), F32)],
        compiler_params=_params(("parallel", "parallel", "arbitrary")))(ik, iq_hm, w_hm)


def _select_kernel(s_ref, idx_ref, key_ref, c_ref, *, tl, rb, topk):
    qi = pl.program_id(1)
    t_lane = qi * tl + lax.broadcasted_iota(I32, (1, tl), 1)
    nblk = ((qi + 1) * tl + rb - 1) // rb
    sub = rb // SUBLANES

    def rows_of(r):
        return pl.ds(pl.multiple_of(r * rb, rb), rb)

    def row_ids(r):
        return r * rb + lax.broadcasted_iota(I32, (rb, tl), 0)

    def fold(m):
        return jnp.sum(m.reshape(sub, SUBLANES, tl), axis=0)

    def make_keys(r, carry):
        x = s_ref[0, rows_of(r), :]
        x = jnp.where(x == 0.0, 0.0, x)
        bits = lax.bitcast_convert_type(x, I32)
        key = bits ^ ((bits >> 31) & 0x7FFFFFFF)
        key_ref[rows_of(r), :] = jnp.where(row_ids(r) <= t_lane, key, INT_MIN)
        return carry

    lax.fori_loop(0, nblk, make_keys, 0)

    def count_ge(cand):
        def body(r, acc):
            return acc + fold((key_ref[rows_of(r), :] >= cand).astype(I32))
        acc = lax.fori_loop(0, nblk, body, jnp.zeros((SUBLANES, tl), I32))
        return jnp.sum(acc, axis=0, keepdims=True)

    zero = jnp.zeros((1, tl), I32)
    thr = jnp.where(count_ge(zero) >= topk, zero, jnp.full((1, tl), INT_MIN, I32))

    def bit_step(i, thr):
        cand = thr | (jnp.int32(1) << (30 - i))
        return jnp.where(count_ge(cand) >= topk, cand, thr)

    thr = lax.fori_loop(0, 31, bit_step, thr)

    def count_gt(r, acc):
        return acc + fold((key_ref[rows_of(r), :] > thr).astype(I32))

    n_gt = jnp.sum(lax.fori_loop(0, nblk, count_gt, jnp.zeros((SUBLANES, tl), I32)), axis=0, keepdims=True)
    need = (topk - n_gt).astype(F32)

    ri = lax.broadcasted_iota(I32, (rb, rb), 0)
    ci = lax.broadcasted_iota(I32, (rb, rb), 1)
    tri = (ci <= ri).astype(BF16)

    def prefix(r, carry):
        carry_eq, carry_c = carry
        key = key_ref[rows_of(r), :]
        eq = jnp.logical_and(key == thr, row_ids(r) <= t_lane)
        ceq = jnp.dot(tri, eq.astype(BF16), preferred_element_type=F32) + carry_eq
        sel = jnp.logical_or(key > thr, jnp.logical_and(eq, ceq <= need))
        c = jnp.dot(tri, sel.astype(BF16), preferred_element_type=F32) + carry_c
        c_ref[rows_of(r), :] = c
        return ceq[rb - 1:rb, :], c[rb - 1:rb, :]

    zf = jnp.zeros((1, tl), F32)
    lax.fori_loop(0, nblk, prefix, (zf, zf))

    group = SUBLANES

    def slots(jb, carry):
        def body(r, accs):
            c = c_ref[rows_of(r), :]
            return tuple(a + fold((c <= (jb * group + u).astype(F32)).astype(F32)) for u, a in enumerate(accs))
        accs = lax.fori_loop(0, nblk, body, tuple(jnp.zeros((SUBLANES, tl), F32) for _ in range(group)))
        out = jnp.concatenate([jnp.sum(a, axis=0, keepdims=True) for a in accs], axis=0)
        out = jnp.minimum(out.astype(I32), s_ref.shape[1] - 1)
        idx_ref[0, pl.ds(pl.multiple_of(jb * group, group), group), :] = out
        return carry

    lax.fori_loop(0, topk // group, slots, 0)


def select_topk(s_t, topk):
    b, s, _ = s_t.shape
    tl = _tile(s, LANES)
    rb = _tile(s, 256)
    return pl.pallas_call(
        functools.partial(_select_kernel, tl=tl, rb=rb, topk=topk), name="select_topk",
        grid=(b, s // tl),
        in_specs=[pl.BlockSpec((1, s, tl), lambda bi, qi: (bi, 0, qi))],
        out_specs=pl.BlockSpec((1, topk, tl), lambda bi, qi: (bi, 0, qi)),
        out_shape=jax.ShapeDtypeStruct((b, topk, s), I32),
        scratch_shapes=[pltpu.VMEM((s, tl), I32), pltpu.VMEM((s, tl), F32)],
        compiler_params=_params(("parallel", "parallel")))(s_t)


HI16 = -65536
QUERY_BATCH = 4


def _pack_kv_kernel(k_ref, v_ref, o_ref):
    kb = lax.bitcast_convert_type(k_ref[...].astype(BF16).astype(F32), I32)
    vb = lax.bitcast_convert_type(v_ref[...].astype(BF16).astype(F32), I32)
    o_ref[...] = (vb & HI16) | ((kb >> 16) & 0xFFFF)


def pack_kv(k, v):
    n, w = k.shape
    tr = _tile(n, 1024)
    spec = pl.BlockSpec((tr, w), lambda i: (i, 0))
    return pl.pallas_call(
        _pack_kv_kernel, name="pack_kv", grid=(n // tr,), in_specs=[spec, spec], out_specs=spec,
        out_shape=jax.ShapeDtypeStruct((n, w), I32),
        compiler_params=_params(("parallel",)))(k, v)


def _sparse_attn_kernel(idx_ref, q_ref, kv_ref, o_ref, st_ref, *, tq, topk):
    qt = pl.program_id(1)
    scale = HEAD_DIM ** -0.5

    def per_query(qq, carry):
        def gather(i, c):
            base = pl.multiple_of(i * SUBLANES, SUBLANES)
            dst = st_ref.at[qq, pl.ds(base, SUBLANES)]
            for u in range(SUBLANES):
                r = idx_ref[0, 0, qq * topk + base + u]
                dst[u:u + 1, :] = kv_ref[0, pl.ds(r, 1), :]
            return c

        return lax.fori_loop(0, topk // SUBLANES, gather, carry)

    lax.fori_loop(0, tq, per_query, 0)

    slot = lax.broadcasted_iota(I32, (QUERY_BATCH, KV_GROUP, topk), 2)
    qoff = lax.broadcasted_iota(I32, (QUERY_BATCH, KV_GROUP, topk), 0)

    def attend(qb, carry):
        rows = pl.ds(pl.multiple_of(qb * QUERY_BATCH, QUERY_BATCH), QUERY_BATCH)
        valid = slot < jnp.minimum(qt * tq + qb * QUERY_BATCH + qoff + 1, topk)
        for h in range(N_KV_HEADS):
            x = st_ref[rows, :, h * HEAD_DIM:(h + 1) * HEAD_DIM]
            kb = lax.bitcast_convert_type(x << 16, F32).astype(BF16)
            vb = lax.bitcast_convert_type(x & HI16, F32).astype(BF16)
            s = jnp.einsum('qgd,qjd->qgj', q_ref[0, rows, h], kb, preferred_element_type=F32) * scale
            s = jnp.where(valid, s, -jnp.inf)
            e = jnp.exp(s - jnp.max(s, axis=-1, keepdims=True))
            p = e / jnp.sum(e, axis=-1, keepdims=True)
            o_ref[0, rows, h] = jnp.einsum('qgj,qjd->qgd', p.astype(BF16), vb, preferred_element_type=F32)
        return carry

    lax.fori_loop(0, tq // QUERY_BATCH, attend, 0)


def sparse_attention(idx, q, kv, topk):
    b, s, _ = idx.shape
    w = kv.shape[2]
    tq = _tile(s, 16)
    nqt = s // tq
    idx = idx.reshape(b * nqt, 1, tq * topk)
    return pl.pallas_call(
        functools.partial(_sparse_attn_kernel, tq=tq, topk=topk), name="sparse_attention",
        grid=(b, nqt),
        in_specs=[pl.BlockSpec((1, 1, tq * topk), lambda bi, qt: (bi * nqt + qt, 0, 0), memory_space=pltpu.SMEM),
                  pl.BlockSpec((1, tq, N_KV_HEADS, KV_GROUP, HEAD_DIM), lambda bi, qt: (bi, qt, 0, 0, 0)),
                  pl.BlockSpec((1, s, w), lambda bi, qt: (bi, 0, 0), pipeline_mode=pl.Buffered(1))],
        out_specs=pl.BlockSpec((1, tq, N_KV_HEADS, KV_GROUP, HEAD_DIM), lambda bi, qt: (bi, qt, 0, 0, 0)),
        out_shape=jax.ShapeDtypeStruct((b, s, N_KV_HEADS, KV_GROUP, HEAD_DIM), F32),
        scratch_shapes=[pltpu.VMEM((tq, topk, w), I32)],
        compiler_params=_params(("parallel", "arbitrary")))(idx, q, kv)


def dsa_mixer(xn, w_in, w_out, positions):
    b, s, d = xn.shape
    n = b * s
    q_cols = d
    kv_cols = N_KV_HEADS * HEAD_DIM
    iq_cols = IDX_HEADS * IDX_DIM
    o0 = 0
    bounds = []
    for width in (q_cols, kv_cols, kv_cols, iq_cols, IDX_DIM, IDX_HEADS):
        bounds.append((o0, o0 + width))
        o0 += width
    w_bf = w_in.astype(BF16)
    x2 = xn.reshape(n, d)
    q, k, v, iq, ik, iw = (matmul(x2, w_bf[:, lo:hi]).reshape(b, s, hi - lo) for lo, hi in bounds)

    inv = ROPE_THETA ** (-jnp.arange(0, ROT_DIM, 2, dtype=F32) / ROT_DIM)
    inv_lane = jnp.tile(inv, LANES // inv.shape[0]).reshape(1, LANES)
    pos = positions.reshape(b, s, 1)
    tq = _tile(s, 256)
    tk = _tile(s, 512)
    q_r = rope(q, pos, inv_lane, d // HEAD_DIM, BF16)
    k_r = rope(k, pos, inv_lane, N_KV_HEADS, F32)
    iq_hm = rope(iq, pos, inv_lane, IDX_HEADS, BF16, head_major_tile=tq)
    ik_r = rope(ik, pos, inv_lane, 1, BF16)
    w_hm = iw.reshape(b, s // tq, tq, IDX_HEADS).transpose(0, 1, 3, 2)

    topk = min(INDEX_TOPK, s // 4)
    s_t = index_scores(ik_r, iq_hm.reshape(b, s // tq, IDX_HEADS * tq, IDX_DIM), w_hm, tq, tk)
    idx = select_topk(s_t, topk).transpose(0, 2, 1)
    kv = pack_kv(k_r.reshape(n, kv_cols), v.reshape(n, kv_cols)).reshape(b, s, kv_cols)
    o = sparse_attention(idx, q_r.reshape(b, s, N_KV_HEADS, KV_GROUP, HEAD_DIM), kv, topk)
    return matmul(o.reshape(n, d), w_out.astype(BF16))


def memory_xattn(hn, mem_n, wq, wkv, wo, b, s):
    width = wq.shape[1]
    m = mem_n.shape[0] // b
    q = matmul(hn, wq.astype(BF16), out_dtype=BF16).reshape(b, s, width)
    kv = matmul(mem_n, wkv.astype(BF16), out_dtype=BF16)
    km = kv[:, :width].reshape(b, m, width)
    vm = kv[:, width:].reshape(b, m, width)
    o = xattn_core(q, km, vm)
    return matmul(o.reshape(b * s, width), wo.astype(BF16))


def kernel(x, mem, positions, norm_gains, mem_norm, pool_w, pool_scale, dsa_w_in, dsa_w_out,
           xattn_wq, xattn_wkv, xattn_wo, ffn_w_gate_up, ffn_w_down):
    b, s, d = x.shape
    n = b * s
    depth = norm_gains.shape[0]
    mem_n = rms_norm(mem.reshape(-1, d), mem_norm, out_dtype=BF16)
    h = x.reshape(n, d)
    for i in range(depth):
        g = norm_gains[i]
        if i % 2 == 0:
            a = rms_norm(h, g[0])
            a = pool_mixer(a.reshape(b, s, d), pool_w[i // 2], pool_scale[i // 2]).reshape(n, d)
        else:
            a = rms_norm(h, g[0], out_dtype=BF16)
            a = dsa_mixer(a.reshape(b, s, d), dsa_w_in[i // 2], dsa_w_out[i // 2], positions)
        h = rms_norm(a, g[1], res=h)
        c = memory_xattn(rms_norm(h, g[2], out_dtype=BF16), mem_n, xattn_wq[i], xattn_wkv[i], xattn_wo[i], b, s)
        h = rms_norm(c, g[3], res=h)
        act = swiglu_up(rms_norm(h, g[4], out_dtype=BF16), ffn_w_gate_up[i].astype(BF16))
        f = matmul(act, ffn_w_down[i].astype(BF16))
        h = rms_norm(f, g[5], res=h)
    return h.reshape(b, s, d)
```

```python
import functools

import jax
import jax.numpy as jnp
from jax import lax
from jax.experimental import pallas as pl
from jax.experimental.pallas import tpu as pltpu

F32 = jnp.float32
BF16 = jnp.bfloat16
I32 = jnp.int32

LANES = 128
SUBLANES = 8
VMEM_LIMIT_BYTES = 56 * 1024 * 1024

NORM_EPS = 1e-6
POOL_WINDOWS = (2, 4, 8, 16)
MAX_WINDOW = 16
HEAD_DIM = 128
N_KV_HEADS = 4
KV_GROUP = 4
IDX_HEADS = 16
IDX_DIM = 128
INDEX_TOPK = 256
ROPE_THETA = 500000.0
ROT_DIM = 32
XATTN_HEADS = 4
XATTN_DIM = 128
INT_MIN = -(2 ** 31)


def _params(semantics):
    return pltpu.CompilerParams(dimension_semantics=semantics, vmem_limit_bytes=VMEM_LIMIT_BYTES)


def _tile(n, want):
    t = min(n, want)
    while n % t:
        t //= 2
    return t


def _norm_body(x_ref, g_ref):
    x = x_ref[...].astype(F32)
    ms = jnp.mean(x * x, axis=-1, keepdims=True)
    return x * lax.rsqrt(ms + NORM_EPS) * g_ref[...]


def _norm_kernel(x_ref, g_ref, o_ref):
    o_ref[...] = _norm_body(x_ref, g_ref).astype(o_ref.dtype)


def _norm_res_kernel(x_ref, g_ref, r_ref, o_ref):
    o_ref[...] = (r_ref[...] + _norm_body(x_ref, g_ref)).astype(o_ref.dtype)


def rms_norm(x, g, res=None, out_dtype=F32):
    n, d = x.shape
    tr = _tile(n, 256)
    row = pl.BlockSpec((tr, d), lambda i: (i, 0))
    gain = pl.BlockSpec((1, d), lambda i: (0, 0))
    g2 = g.reshape(1, d).astype(F32)
    if res is None:
        kern, specs, args = _norm_kernel, [row, gain], (x, g2)
    else:
        kern, specs, args = _norm_res_kernel, [row, gain, row], (x, g2, res)
    return pl.pallas_call(
        kern, name="rms_norm", grid=(n // tr,), in_specs=specs, out_specs=row,
        out_shape=jax.ShapeDtypeStruct((n, d), out_dtype),
        compiler_params=_params(("parallel",)))(*args)


def _mm_kernel(a_ref, b_ref, o_ref, acc_ref, *, nk):
    k = pl.program_id(2)

    @pl.when(k == 0)
    def _():
        acc_ref[...] = jnp.zeros_like(acc_ref)

    acc_ref[...] += jnp.dot(a_ref[...].astype(BF16), b_ref[...].astype(BF16),
                            preferred_element_type=F32)

    @pl.when(k == nk - 1)
    def _():
        o_ref[...] = acc_ref[...].astype(o_ref.dtype)


def matmul(a, b, out_dtype=F32, tm=512, tn=512, tk=1024):
    m, kd = a.shape
    n = b.shape[1]
    tm, tn, tk = _tile(m, tm), _tile(n, tn), _tile(kd, tk)
    nk = kd // tk
    return pl.pallas_call(
        functools.partial(_mm_kernel, nk=nk), name="matmul",
        grid=(m // tm, n // tn, nk),
        in_specs=[pl.BlockSpec((tm, tk), lambda i, j, k: (i, k)),
                  pl.BlockSpec((tk, tn), lambda i, j, k: (k, j))],
        out_specs=pl.BlockSpec((tm, tn), lambda i, j, k: (i, j)),
        out_shape=jax.ShapeDtypeStruct((m, n), out_dtype),
        scratch_shapes=[pltpu.VMEM((tm, tn), F32)],
        compiler_params=_params(("parallel", "parallel", "arbitrary")))(a, b)


def _swiglu_kernel(a_ref, bg_ref, bu_ref, o_ref):
    a = a_ref[...].astype(BF16)
    g = jnp.dot(a, bg_ref[...].astype(BF16), preferred_element_type=F32)
    u = jnp.dot(a, bu_ref[...].astype(BF16), preferred_element_type=F32)
    o_ref[...] = (g * (1.0 / (1.0 + jnp.exp(-g))) * u).astype(o_ref.dtype)


def swiglu_up(a, w_gate_up, tm=512, tn=512):
    m, kd = a.shape
    f = w_gate_up.shape[1] // 2
    tm, tn = _tile(m, tm), _tile(f, tn)
    nf = f // tn
    return pl.pallas_call(
        _swiglu_kernel, name="swiglu_up",
        grid=(m // tm, nf),
        in_specs=[pl.BlockSpec((tm, kd), lambda i, j: (i, 0)),
                  pl.BlockSpec((kd, tn), lambda i, j: (0, j)),
                  pl.BlockSpec((kd, tn), lambda i, j: (0, j + nf))],
        out_specs=pl.BlockSpec((tm, tn), lambda i, j: (i, j)),
        out_shape=jax.ShapeDtypeStruct((m, f), BF16),
        compiler_params=_params(("parallel", "parallel")))(a, w_gate_up, w_gate_up)


def _pool_kernel(x_ref, halo_ref, w_ref, scale_ref, o_ref, pad_ref, *, ts, cg):
    i = pl.program_id(1)
    row = i * ts + lax.broadcasted_iota(I32, (ts, 1), 0)
    for g, win in enumerate(POOL_WINDOWS):
        cols = slice(g * cg, (g + 1) * cg)
        x = x_ref[0, :, cols]
        halo = halo_ref[0, :, cols]
        pad_ref[0:MAX_WINDOW, :] = jnp.where(i == 0, jnp.zeros_like(halo), halo)
        pad_ref[MAX_WINDOW:, :] = x
        wsum = x
        for k in range(1, win):
            wsum = wsum + pad_ref[MAX_WINDOW - k:MAX_WINDOW - k + ts, :]
        cnt = jnp.minimum(row + 1, win).astype(F32)
        p = wsum / cnt - x
        y = jnp.dot(p.astype(BF16), w_ref[g].astype(BF16), preferred_element_type=F32)
        o_ref[0, :, cols] = y * scale_ref[:, cols]


def pool_mixer(xn, w_group, scale):
    b, s, d = xn.shape
    g, cg, _ = w_group.shape
    ts = _tile(s, 512)
    hb = ts // MAX_WINDOW
    return pl.pallas_call(
        functools.partial(_pool_kernel, ts=ts, cg=cg), name="pool_mixer",
        grid=(b, s // ts),
        in_specs=[pl.BlockSpec((1, ts, d), lambda bi, i: (bi, i, 0)),
                  pl.BlockSpec((1, MAX_WINDOW, d), lambda bi, i: (bi, jnp.maximum(i * hb - 1, 0), 0)),
                  pl.BlockSpec((g, cg, cg), lambda bi, i: (0, 0, 0)),
                  pl.BlockSpec((1, d), lambda bi, i: (0, 0))],
        out_specs=pl.BlockSpec((1, ts, d), lambda bi, i: (bi, i, 0)),
        out_shape=jax.ShapeDtypeStruct((b, s, d), F32),
        scratch_shapes=[pltpu.VMEM((MAX_WINDOW + ts, cg), F32)],
        compiler_params=_params(("parallel", "arbitrary")))(xn, xn, w_group.astype(BF16), scale.reshape(1, d))


def _xattn_kernel(q_ref, k_ref, v_ref, o_ref):
    scale = XATTN_DIM ** -0.5
    for h in range(XATTN_HEADS):
        cols = slice(h * XATTN_DIM, (h + 1) * XATTN_DIM)
        q = q_ref[0, :, cols].astype(BF16)
        k = k_ref[0, :, cols].astype(BF16)
        v = v_ref[0, :, cols].astype(BF16)
        s = lax.dot_general(q, k, (((1,), (1,)), ((), ())), preferred_element_type=F32) * scale
        e = jnp.exp(s - jnp.max(s, axis=-1, keepdims=True))
        p = e / jnp.sum(e, axis=-1, keepdims=True)
        o_ref[0, :, cols] = jnp.dot(p.astype(BF16), v, preferred_element_type=F32).astype(o_ref.dtype)


def xattn_core(q, km, vm):
    b, s, w = q.shape
    m = km.shape[1]
    ts = _tile(s, 512)
    return pl.pallas_call(
        _xattn_kernel, name="xattn_core",
        grid=(b, s // ts),
        in_specs=[pl.BlockSpec((1, ts, w), lambda bi, i: (bi, i, 0)),
                  pl.BlockSpec((1, m, w), lambda bi, i: (bi, 0, 0)),
                  pl.BlockSpec((1, m, w), lambda bi, i: (bi, 0, 0))],
        out_specs=pl.BlockSpec((1, ts, w), lambda bi, i: (bi, i, 0)),
        out_shape=jax.ShapeDtypeStruct((b, s, w), BF16),
        compiler_params=_params(("parallel", "parallel")))(q, km, vm)


def _rope_kernel(x_ref, pos_ref, inv_ref, o_ref, *, head_major):
    x = x_ref[0].astype(F32)
    ang = pos_ref[0].astype(F32) * inv_ref[...]
    lane = lax.broadcasted_iota(I32, x.shape, 1)
    half = ROT_DIM // 2
    c = jnp.where(lane < ROT_DIM, jnp.cos(ang), 1.0)
    sn = jnp.sin(ang)
    sgn = jnp.where(lane < half, -sn, jnp.where(lane < ROT_DIM, sn, 0.0))
    partner = jnp.where(lane < half, pltpu.roll(x, LANES - half, 1), pltpu.roll(x, half, 1))
    y = (x * c + partner * sgn).astype(o_ref.dtype)
    if head_major:
        o_ref[0, 0, 0] = y
    else:
        o_ref[0] = y


def rope(x, pos, inv_lane, n_heads, out_dtype, head_major_tile=None):
    b, s, _ = x.shape
    ts = head_major_tile or _tile(s, 512)
    if head_major_tile:
        out_shape = jax.ShapeDtypeStruct((b, s // ts, n_heads, ts, HEAD_DIM), out_dtype)
        out_spec = pl.BlockSpec((1, 1, 1, ts, HEAD_DIM), lambda bi, i, h: (bi, i, h, 0, 0))
    else:
        out_shape = jax.ShapeDtypeStruct(x.shape, out_dtype)
        out_spec = pl.BlockSpec((1, ts, HEAD_DIM), lambda bi, i, h: (bi, i, h))
    return pl.pallas_call(
        functools.partial(_rope_kernel, head_major=bool(head_major_tile)), name="rope",
        grid=(b, s // ts, n_heads),
        in_specs=[pl.BlockSpec((1, ts, HEAD_DIM), lambda bi, i, h: (bi, i, h)),
                  pl.BlockSpec((1, ts, 1), lambda bi, i, h: (bi, i, 0)),
                  pl.BlockSpec((1, HEAD_DIM), lambda bi, i, h: (0, 0))],
        out_specs=out_spec, out_shape=out_shape,
        compiler_params=_params(("parallel", "parallel", "parallel")))(x, pos, inv_lane)


def _index_kernel(ik_ref, iq_ref, w_ref, o_ref, x_ref, *, tq, tk, rc):
    qi = pl.program_id(1)
    kj = pl.program_id(2)
    needed = kj * tk <= qi * tq + tq - 1

    @pl.when(needed)
    def _():
        x_ref[...] = lax.dot_general(ik_ref[0], iq_ref[0, 0], (((1,), (1,)), ((), ())),
                                     preferred_element_type=F32)
        w = w_ref[0, 0] * (IDX_HEADS ** -0.5 * IDX_DIM ** -0.5)

        def chunk(r, carry):
            rows = pl.ds(pl.multiple_of(r * rc, rc), rc)
            acc = jnp.zeros((rc, tq), F32)
            for h in range(IDX_HEADS):
                acc = acc + jnp.maximum(x_ref[rows, h * tq:(h + 1) * tq], 0.0) * w[h:h + 1, :]
            o_ref[0, rows, :] = acc
            return carry

        lax.fori_loop(0, tk // rc, chunk, 0)

    @pl.when(jnp.logical_not(needed))
    def _():
        o_ref[...] = jnp.zeros_like(o_ref)


def index_scores(ik, iq_hm, w_hm, tq, tk):
    b, s, _ = ik.shape
    nq, nk = s // tq, s // tk

    def last_needed(qi):
        return (qi * tq + tq - 1) // tk

    return pl.pallas_call(
        functools.partial(_index_kernel, tq=tq, tk=tk, rc=min(tk, 32)), name="index_scores",
        grid=(b, nq, nk),
        in_specs=[pl.BlockSpec((1, tk, IDX_DIM), lambda bi, qi, kj: (bi, jnp.minimum(kj, last_needed(qi)), 0)),
                  pl.BlockSpec((1, 1, IDX_HEADS * tq, IDX_DIM), lambda bi, qi, kj: (bi, qi, 0, 0)),
                  pl.BlockSpec((1, 1, IDX_HEADS, tq), lambda bi, qi, kj: (bi, qi, 0, 0))],
        out_specs=pl.BlockSpec((1, tk, tq), lambda bi, qi, kj: (bi, kj, qi)),
        out_shape=jax.ShapeDtypeStruct((b, s, s), F32),
        scratch_shapes=[pltpu.VMEM((tk, IDX_HEADS * tq), F32)],
        compiler_params=_params(("parallel", "parallel", "arbitrary")))(ik, iq_hm, w_hm)


def _select_kernel(s_ref, idx_ref, key_ref, c_ref, acc_ref, *, tl, rb, topk, row_stride):
    qi = pl.program_id(1)
    t_lane = qi * tl + lax.broadcasted_iota(I32, (1, tl), 1)
    nblk = ((qi + 1) * tl + rb - 1) // rb
    sub = rb // SUBLANES

    def rows_of(r):
        return pl.ds(pl.multiple_of(r * rb, rb), rb)

    def row_ids(r):
        return r * rb + lax.broadcasted_iota(I32, (rb, tl), 0)

    def fold(m):
        return jnp.sum(m.reshape(sub, SUBLANES, tl), axis=0)

    def make_keys(r, carry):
        x = s_ref[0, rows_of(r), :]
        x = jnp.where(x == 0.0, 0.0, x)
        bits = lax.bitcast_convert_type(x, I32)
        key = bits ^ ((bits >> 31) & 0x7FFFFFFF)
        key_ref[rows_of(r), :] = jnp.where(row_ids(r) <= t_lane, key, INT_MIN)
        return carry

    lax.fori_loop(0, nblk, make_keys, 0)

    def count_ge(cand):
        def body(r, acc):
            return acc + fold((key_ref[rows_of(r), :] >= cand).astype(I32))
        acc = lax.fori_loop(0, nblk, body, jnp.zeros((SUBLANES, tl), I32))
        return jnp.sum(acc, axis=0, keepdims=True)

    zero = jnp.zeros((1, tl), I32)
    thr = jnp.where(count_ge(zero) >= topk, zero, jnp.full((1, tl), INT_MIN, I32))

    def bit_step(i, thr):
        cand = thr | (jnp.int32(1) << (30 - i))
        return jnp.where(count_ge(cand) >= topk, cand, thr)

    thr = lax.fori_loop(0, 31, bit_step, thr)

    def count_gt(r, acc):
        return acc + fold((key_ref[rows_of(r), :] > thr).astype(I32))

    n_gt = jnp.sum(lax.fori_loop(0, nblk, count_gt, jnp.zeros((SUBLANES, tl), I32)), axis=0, keepdims=True)
    need = (topk - n_gt).astype(F32)

    ri = lax.broadcasted_iota(I32, (rb, rb), 0)
    ci = lax.broadcasted_iota(I32, (rb, rb), 1)
    tri = (ci <= ri).astype(BF16)

    def prefix(r, carry):
        carry_eq, carry_c = carry
        key = key_ref[rows_of(r), :]
        eq = jnp.logical_and(key == thr, row_ids(r) <= t_lane)
        ceq = jnp.dot(tri, eq.astype(BF16), preferred_element_type=F32) + carry_eq
        sel = jnp.logical_or(key > thr, jnp.logical_and(eq, ceq <= need))
        c = jnp.dot(tri, sel.astype(BF16), preferred_element_type=F32) + carry_c
        c_ref[rows_of(r), :] = c
        return ceq[rb - 1:rb, :], c[rb - 1:rb, :]

    zf = jnp.zeros((1, tl), F32)
    lax.fori_loop(0, nblk, prefix, (zf, zf))

    group = SUBLANES
    jrow = lax.broadcasted_iota(I32, (topk, tl), 0)
    acc_ref[...] = jnp.zeros_like(acc_ref)

    def block_slots(r, c_before):
        c = c_ref[rows_of(r), :]
        c_last = c[rb - 1:rb, :]
        g_lo = jnp.min(c_before).astype(I32) // group
        g_hi = (jnp.minimum(jnp.max(c_last).astype(I32), topk) + group - 1) // group

        def slot_group(jg, carry):
            counts = [jnp.sum(fold((c <= (jg * group + u).astype(F32)).astype(F32)), axis=0, keepdims=True)
                      for u in range(group)]
            rows = pl.ds(pl.multiple_of(jg * group, group), group)
            acc_ref[rows, :] += jnp.concatenate(counts, axis=0)
            return carry

        lax.fori_loop(g_lo, g_hi, slot_group, 0)
        acc_ref[...] += jnp.where(jrow >= g_hi * group, float(rb), 0.0)
        return c_last

    lax.fori_loop(0, nblk, block_slots, zf)
    idx_ref[0] = jnp.minimum(acc_ref[...].astype(I32), s_ref.shape[1] - 1) * row_stride


def select_topk(s_t, topk, row_stride=1):
    b, s, _ = s_t.shape
    tl = _tile(s, LANES)
    rb = _tile(s, 256)
    return pl.pallas_call(
        functools.partial(_select_kernel, tl=tl, rb=rb, topk=topk, row_stride=row_stride), name="select_topk",
        grid=(b, s // tl),
        in_specs=[pl.BlockSpec((1, s, tl), lambda bi, qi: (bi, 0, qi))],
        out_specs=pl.BlockSpec((1, topk, tl), lambda bi, qi: (bi, 0, qi)),
        out_shape=jax.ShapeDtypeStruct((b, topk, s), I32),
        scratch_shapes=[pltpu.VMEM((s, tl), I32), pltpu.VMEM((s, tl), F32), pltpu.VMEM((topk, tl), F32)],
        compiler_params=_params(("parallel", "parallel")))(s_t)


HI16 = -65536
QUERY_BATCH = 16


def _pack_kv_kernel(k_ref, v_ref, o_ref):
    kb = lax.bitcast_convert_type(k_ref[...].astype(BF16).astype(F32), I32)
    vb = lax.bitcast_convert_type(v_ref[...].astype(BF16).astype(F32), I32)
    o_ref[...] = (vb & HI16) | ((kb >> 16) & 0xFFFF)


def pack_kv(k, v):
    n, w = k.shape
    tr = _tile(n, 1024)
    spec = pl.BlockSpec((tr, w), lambda i: (i, 0))
    return pl.pallas_call(
        _pack_kv_kernel, name="pack_kv", grid=(n // tr,), in_specs=[spec, spec], out_specs=spec,
        out_shape=jax.ShapeDtypeStruct((n, w), I32),
        compiler_params=_params(("parallel",)))(k, v)


def _sparse_attn_kernel(idx_ref, q_ref, kv_ref, o_ref, st_ref, *, tq, topk):
    qt = pl.program_id(1)
    scale = HEAD_DIM ** -0.5
    rows_per_iter = SUBLANES * N_KV_HEADS

    def per_query(qq, carry):
        def gather(i, c):
            base = pl.multiple_of(i * SUBLANES, SUBLANES)
            dst = st_ref.at[qq, pl.ds(pl.multiple_of(i * rows_per_iter, rows_per_iter), rows_per_iter)]
            for u in range(SUBLANES):
                r = pl.multiple_of(idx_ref[0, 0, qq * topk + base + u], N_KV_HEADS)
                dst[u * N_KV_HEADS:(u + 1) * N_KV_HEADS, :] = kv_ref[0, pl.ds(r, N_KV_HEADS), :]
            return c

        return lax.fori_loop(0, topk // SUBLANES, gather, carry)

    lax.fori_loop(0, tq, per_query, 0)

    slot = lax.broadcasted_iota(I32, (QUERY_BATCH, KV_GROUP, topk), 2)
    qoff = lax.broadcasted_iota(I32, (QUERY_BATCH, KV_GROUP, topk), 0)

    def attend(qb, carry):
        rows = pl.ds(pl.multiple_of(qb * QUERY_BATCH, QUERY_BATCH), QUERY_BATCH)
        valid = slot < jnp.minimum(qt * tq + qb * QUERY_BATCH + qoff + 1, topk)
        for h in range(N_KV_HEADS):
            x = st_ref[rows, pl.ds(h, topk, stride=N_KV_HEADS), :]
            kb = lax.bitcast_convert_type(x << 16, F32).astype(BF16)
            vb = lax.bitcast_convert_type(x & HI16, F32).astype(BF16)
            s = jnp.einsum('qgd,qjd->qgj', q_ref[0, rows, h], kb, preferred_element_type=F32) * scale
            s = jnp.where(valid, s, -jnp.inf)
            e = jnp.exp(s - jnp.max(s, axis=-1, keepdims=True))
            p = e / jnp.sum(e, axis=-1, keepdims=True)
            o_ref[0, rows, h] = jnp.einsum('qgj,qjd->qgd', p.astype(BF16), vb, preferred_element_type=F32)
        return carry

    lax.fori_loop(0, tq // QUERY_BATCH, attend, 0)


def sparse_attention(idx, q, kv, topk):
    b, s, _ = idx.shape
    tq = _tile(s, 16)
    nqt = s // tq
    idx = idx.reshape(b * nqt, 1, tq * topk)
    return pl.pallas_call(
        functools.partial(_sparse_attn_kernel, tq=tq, topk=topk), name="sparse_attention",
        grid=(b, nqt),
        in_specs=[pl.BlockSpec((1, 1, tq * topk), lambda bi, qt: (bi * nqt + qt, 0, 0), memory_space=pltpu.SMEM),
                  pl.BlockSpec((1, tq, N_KV_HEADS, KV_GROUP, HEAD_DIM), lambda bi, qt: (bi, qt, 0, 0, 0)),
                  pl.BlockSpec((1, s * N_KV_HEADS, HEAD_DIM), lambda bi, qt: (bi, 0, 0),
                               pipeline_mode=pl.Buffered(1))],
        out_specs=pl.BlockSpec((1, tq, N_KV_HEADS, KV_GROUP, HEAD_DIM), lambda bi, qt: (bi, qt, 0, 0, 0)),
        out_shape=jax.ShapeDtypeStruct((b, s, N_KV_HEADS, KV_GROUP, HEAD_DIM), F32),
        scratch_shapes=[pltpu.VMEM((tq, topk * N_KV_HEADS, HEAD_DIM), I32)],
        compiler_params=_params(("parallel", "arbitrary")))(idx, q, kv)


def dsa_mixer(xn, w_in, w_out, positions):
    b, s, d = xn.shape
    n = b * s
    q_cols = d
    kv_cols = N_KV_HEADS * HEAD_DIM
    iq_cols = IDX_HEADS * IDX_DIM
    o0 = 0
    bounds = []
    for width in (q_cols, kv_cols, kv_cols, iq_cols, IDX_DIM, IDX_HEADS):
        bounds.append((o0, o0 + width))
        o0 += width
    w_bf = w_in.astype(BF16)
    x2 = xn.reshape(n, d)
    q, k, v, iq, ik, iw = (matmul(x2, w_bf[:, lo:hi]).reshape(b, s, hi - lo) for lo, hi in bounds)

    inv = ROPE_THETA ** (-jnp.arange(0, ROT_DIM, 2, dtype=F32) / ROT_DIM)
    inv_lane = jnp.tile(inv, LANES // inv.shape[0]).reshape(1, LANES)
    pos = positions.reshape(b, s, 1)
    tq = _tile(s, 256)
    tk = _tile(s, 512)
    q_r = rope(q, pos, inv_lane, d // HEAD_DIM, BF16)
    k_r = rope(k, pos, inv_lane, N_KV_HEADS, F32)
    iq_hm = rope(iq, pos, inv_lane, IDX_HEADS, BF16, head_major_tile=tq)
    ik_r = rope(ik, pos, inv_lane, 1, BF16)
    w_hm = iw.reshape(b, s // tq, tq, IDX_HEADS).transpose(0, 1, 3, 2)

    topk = min(INDEX_TOPK, s // 4)
    s_t = index_scores(ik_r, iq_hm.reshape(b, s // tq, IDX_HEADS * tq, IDX_DIM), w_hm, tq, tk)
    idx = select_topk(s_t, topk, row_stride=N_KV_HEADS).transpose(0, 2, 1)
    kv = pack_kv(k_r.reshape(n, kv_cols), v.reshape(n, kv_cols)).reshape(b, s * N_KV_HEADS, HEAD_DIM)
    o = sparse_attention(idx, q_r.reshape(b, s, N_KV_HEADS, KV_GROUP, HEAD_DIM), kv, topk)
    return matmul(o.reshape(n, d), w_out.astype(BF16))


def memory_xattn(hn, mem_n, wq, wkv, wo, b, s):
    width = wq.shape[1]
    m = mem_n.shape[0] // b
    q = matmul(hn, wq.astype(BF16), out_dtype=BF16).reshape(b, s, width)
    kv = matmul(mem_n, wkv.astype(BF16), out_dtype=BF16)
    km = kv[:, :width].reshape(b, m, width)
    vm = kv[:, width:].reshape(b, m, width)
    o = xattn_core(q, km, vm)
    return matmul(o.reshape(b * s, width), wo.astype(BF16))


def kernel(x, mem, positions, norm_gains, mem_norm, pool_w, pool_scale, dsa_w_in, dsa_w_out,
           xattn_wq, xattn_wkv, xattn_wo, ffn_w_gate_up, ffn_w_down):
    b, s, d = x.shape
    n = b * s
    depth = norm_gains.shape[0]
    mem_n = rms_norm(mem.reshape(-1, d), mem_norm, out_dtype=BF16)
    h = x.reshape(n, d)
    for i in range(depth):
        g = norm_gains[i]
        if i % 2 == 0:
            a = rms_norm(h, g[0])
            a = pool_mixer(a.reshape(b, s, d), pool_w[i // 2], pool_scale[i // 2]).reshape(n, d)
        else:
            a = rms_norm(h, g[0], out_dtype=BF16)
            a = dsa_mixer(a.reshape(b, s, d), dsa_w_in[i // 2], dsa_w_out[i // 2], positions)
        h = rms_norm(a, g[1], res=h)
        c = memory_xattn(rms_norm(h, g[2], out_dtype=BF16), mem_n, xattn_wq[i], xattn_wkv[i], xattn_wo[i], b, s)
        h = rms_norm(c, g[3], res=h)
        act = swiglu_up(rms_norm(h, g[4], out_dtype=BF16), ffn_w_gate_up[i].astype(BF16))
        f = matmul(act, ffn_w_down[i].astype(BF16))
        h = rms_norm(f, g[5], res=h)
    return h.reshape(b, s, d)
```

```python
import functools

import jax
import jax.numpy as jnp
from jax import lax
from jax.experimental import pallas as pl
from jax.experimental.pallas import tpu as pltpu

F32 = jnp.float32
BF16 = jnp.bfloat16
I32 = jnp.int32

LANES = 128
SUBLANES = 8
VMEM_LIMIT_BYTES = 56 * 1024 * 1024

NORM_EPS = 1e-6
POOL_WINDOWS = (2, 4, 8, 16)
MAX_WINDOW = 16
HEAD_DIM = 128
N_KV_HEADS = 4
KV_GROUP = 4
IDX_HEADS = 16
IDX_DIM = 128
INDEX_TOPK = 256
ROPE_THETA = 500000.0
ROT_DIM = 32
XATTN_HEADS = 4
XATTN_DIM = 128
INT_MIN = -(2 ** 31)


def _params(semantics):
    return pltpu.CompilerParams(dimension_semantics=semantics, vmem_limit_bytes=VMEM_LIMIT_BYTES)


def _tile(n, want):
    t = min(n, want)
    while n % t:
        t //= 2
    return t


def _norm_body(x_ref, g_ref):
    x = x_ref[...].astype(F32)
    ms = jnp.mean(x * x, axis=-1, keepdims=True)
    return x * lax.rsqrt(ms + NORM_EPS) * g_ref[...]


def _norm_kernel(x_ref, g_ref, o_ref):
    o_ref[...] = _norm_body(x_ref, g_ref).astype(o_ref.dtype)


def _norm_res_kernel(x_ref, g_ref, r_ref, o_ref):
    o_ref[...] = (r_ref[...] + _norm_body(x_ref, g_ref)).astype(o_ref.dtype)


def rms_norm(x, g, res=None, out_dtype=F32):
    n, d = x.shape
    tr = _tile(n, 256)
    row = pl.BlockSpec((tr, d), lambda i: (i, 0))
    gain = pl.BlockSpec((1, d), lambda i: (0, 0))
    g2 = g.reshape(1, d).astype(F32)
    if res is None:
        kern, specs, args = _norm_kernel, [row, gain], (x, g2)
    else:
        kern, specs, args = _norm_res_kernel, [row, gain, row], (x, g2, res)
    return pl.pallas_call(
        kern, name="rms_norm", grid=(n // tr,), in_specs=specs, out_specs=row,
        out_shape=jax.ShapeDtypeStruct((n, d), out_dtype),
        compiler_params=_params(("parallel",)))(*args)


def _mm_kernel(a_ref, b_ref, o_ref):
    o_ref[...] = jnp.dot(a_ref[...].astype(BF16), b_ref[...].astype(BF16),
                         preferred_element_type=F32).astype(o_ref.dtype)


def matmul(a, b, out_dtype=F32, tm=512, tn=512):
    m, kd = a.shape
    n = b.shape[1]
    tm, tn = _tile(m, tm), _tile(n, tn)
    return pl.pallas_call(
        _mm_kernel, name="matmul",
        grid=(m // tm, n // tn),
        in_specs=[pl.BlockSpec((tm, kd), lambda i, j: (i, 0)),
                  pl.BlockSpec((kd, tn), lambda i, j: (0, j))],
        out_specs=pl.BlockSpec((tm, tn), lambda i, j: (i, j)),
        out_shape=jax.ShapeDtypeStruct((m, n), out_dtype),
        compiler_params=_params(("parallel", "parallel")))(a, b)


def _swiglu_kernel(a_ref, bg_ref, bu_ref, o_ref):
    a = a_ref[...].astype(BF16)
    g = jnp.dot(a, bg_ref[...].astype(BF16), preferred_element_type=F32)
    u = jnp.dot(a, bu_ref[...].astype(BF16), preferred_element_type=F32)
    o_ref[...] = (g * (1.0 / (1.0 + jnp.exp(-g))) * u).astype(o_ref.dtype)


def swiglu_up(a, w_gate_up, tm=512, tn=512):
    m, kd = a.shape
    f = w_gate_up.shape[1] // 2
    tm, tn = _tile(m, tm), _tile(f, tn)
    nf = f // tn
    return pl.pallas_call(
        _swiglu_kernel, name="swiglu_up",
        grid=(m // tm, nf),
        in_specs=[pl.BlockSpec((tm, kd), lambda i, j: (i, 0)),
                  pl.BlockSpec((kd, tn), lambda i, j: (0, j)),
                  pl.BlockSpec((kd, tn), lambda i, j: (0, j + nf))],
        out_specs=pl.BlockSpec((tm, tn), lambda i, j: (i, j)),
        out_shape=jax.ShapeDtypeStruct((m, f), BF16),
        compiler_params=_params(("parallel", "parallel")))(a, w_gate_up, w_gate_up)


def _pool_kernel(x_ref, halo_ref, w_ref, scale_ref, o_ref, pad_ref, *, ts, cg):
    i = pl.program_id(1)
    row = i * ts + lax.broadcasted_iota(I32, (ts, 1), 0)
    for g, win in enumerate(POOL_WINDOWS):
        cols = slice(g * cg, (g + 1) * cg)
        x = x_ref[0, :, cols]
        halo = halo_ref[0, :, cols]
        pad_ref[0:MAX_WINDOW, :] = jnp.where(i == 0, jnp.zeros_like(halo), halo)
        pad_ref[MAX_WINDOW:, :] = x
        wsum = x
        for k in range(1, win):
            wsum = wsum + pad_ref[MAX_WINDOW - k:MAX_WINDOW - k + ts, :]
        cnt = jnp.minimum(row + 1, win).astype(F32)
        p = wsum / cnt - x
        y = jnp.dot(p.astype(BF16), w_ref[g].astype(BF16), preferred_element_type=F32)
        o_ref[0, :, cols] = y * scale_ref[:, cols]


def pool_mixer(xn, w_group, scale):
    b, s, d = xn.shape
    g, cg, _ = w_group.shape
    ts = _tile(s, 512)
    hb = ts // MAX_WINDOW
    return pl.pallas_call(
        functools.partial(_pool_kernel, ts=ts, cg=cg), name="pool_mixer",
        grid=(b, s // ts),
        in_specs=[pl.BlockSpec((1, ts, d), lambda bi, i: (bi, i, 0)),
                  pl.BlockSpec((1, MAX_WINDOW, d), lambda bi, i: (bi, jnp.maximum(i * hb - 1, 0), 0)),
                  pl.BlockSpec((g, cg, cg), lambda bi, i: (0, 0, 0)),
                  pl.BlockSpec((1, d), lambda bi, i: (0, 0))],
        out_specs=pl.BlockSpec((1, ts, d), lambda bi, i: (bi, i, 0)),
        out_shape=jax.ShapeDtypeStruct((b, s, d), F32),
        scratch_shapes=[pltpu.VMEM((MAX_WINDOW + ts, cg), F32)],
        compiler_params=_params(("parallel", "arbitrary")))(xn, xn, w_group.astype(BF16), scale.reshape(1, d))


def _xattn_kernel(q_ref, k_ref, v_ref, o_ref):
    scale = XATTN_DIM ** -0.5
    for h in range(XATTN_HEADS):
        cols = slice(h * XATTN_DIM, (h + 1) * XATTN_DIM)
        q = q_ref[0, :, cols].astype(BF16)
        k = k_ref[0, :, cols].astype(BF16)
        v = v_ref[0, :, cols].astype(BF16)
        s = lax.dot_general(q, k, (((1,), (1,)), ((), ())), preferred_element_type=F32) * scale
        e = jnp.exp(s - jnp.max(s, axis=-1, keepdims=True))
        p = e / jnp.sum(e, axis=-1, keepdims=True)
        o_ref[0, :, cols] = jnp.dot(p.astype(BF16), v, preferred_element_type=F32).astype(o_ref.dtype)


def xattn_core(q, km, vm):
    b, s, w = q.shape
    m = km.shape[1]
    ts = _tile(s, 512)
    return pl.pallas_call(
        _xattn_kernel, name="xattn_core",
        grid=(b, s // ts),
        in_specs=[pl.BlockSpec((1, ts, w), lambda bi, i: (bi, i, 0)),
                  pl.BlockSpec((1, m, w), lambda bi, i: (bi, 0, 0)),
                  pl.BlockSpec((1, m, w), lambda bi, i: (bi, 0, 0))],
        out_specs=pl.BlockSpec((1, ts, w), lambda bi, i: (bi, i, 0)),
        out_shape=jax.ShapeDtypeStruct((b, s, w), BF16),
        compiler_params=_params(("parallel", "parallel")))(q, km, vm)


def _rope_kernel(x_ref, pos_ref, inv_ref, o_ref, cos_ref, sin_ref, *, head_major):
    x = x_ref[0].astype(F32)
    lane = lax.broadcasted_iota(I32, x.shape, 1)
    half = ROT_DIM // 2

    @pl.when(pl.program_id(2) == 0)
    def _():
        ang = pos_ref[0].astype(F32) * inv_ref[...]
        sn = jnp.sin(ang)
        cos_ref[...] = jnp.where(lane < ROT_DIM, jnp.cos(ang), 1.0)
        sin_ref[...] = jnp.where(lane < half, -sn, jnp.where(lane < ROT_DIM, sn, 0.0))

    partner = jnp.where(lane < half, pltpu.roll(x, LANES - half, 1), pltpu.roll(x, half, 1))
    y = (x * cos_ref[...] + partner * sin_ref[...]).astype(o_ref.dtype)
    if head_major:
        o_ref[0, 0, 0] = y
    else:
        o_ref[0] = y


def rope(x, pos, inv_lane, n_heads, out_dtype, head_major_tile=None):
    b, s, _ = x.shape
    ts = head_major_tile or _tile(s, 512)
    if head_major_tile:
        out_shape = jax.ShapeDtypeStruct((b, s // ts, n_heads, ts, HEAD_DIM), out_dtype)
        out_spec = pl.BlockSpec((1, 1, 1, ts, HEAD_DIM), lambda bi, i, h: (bi, i, h, 0, 0))
    else:
        out_shape = jax.ShapeDtypeStruct(x.shape, out_dtype)
        out_spec = pl.BlockSpec((1, ts, HEAD_DIM), lambda bi, i, h: (bi, i, h))
    return pl.pallas_call(
        functools.partial(_rope_kernel, head_major=bool(head_major_tile)), name="rope",
        grid=(b, s // ts, n_heads),
        in_specs=[pl.BlockSpec((1, ts, HEAD_DIM), lambda bi, i, h: (bi, i, h)),
                  pl.BlockSpec((1, ts, 1), lambda bi, i, h: (bi, i, 0)),
                  pl.BlockSpec((1, HEAD_DIM), lambda bi, i, h: (0, 0))],
        out_specs=out_spec, out_shape=out_shape,
        scratch_shapes=[pltpu.VMEM((ts, HEAD_DIM), F32), pltpu.VMEM((ts, HEAD_DIM), F32)],
        compiler_params=_params(("parallel", "parallel", "arbitrary")))(x, pos, inv_lane)


def _index_kernel(ik_ref, iq_ref, w_ref, o_ref, x_ref, *, tq, tk, rc):
    qi = pl.program_id(1)
    kj = pl.program_id(2)
    needed = kj * tk <= qi * tq + tq - 1

    @pl.when(needed)
    def _():
        x_ref[...] = lax.dot_general(ik_ref[0], iq_ref[0, 0], (((1,), (1,)), ((), ())),
                                     preferred_element_type=F32)
        w = w_ref[0, 0] * (IDX_HEADS ** -0.5 * IDX_DIM ** -0.5)

        def chunk(r, carry):
            rows = pl.ds(pl.multiple_of(r * rc, rc), rc)
            acc = jnp.zeros((rc, tq), F32)
            for h in range(IDX_HEADS):
                acc = acc + jnp.maximum(x_ref[rows, h * tq:(h + 1) * tq], 0.0) * w[h:h + 1, :]
            o_ref[0, rows, :] = acc
            return carry

        lax.fori_loop(0, tk // rc, chunk, 0)

    @pl.when(jnp.logical_not(needed))
    def _():
        o_ref[...] = jnp.zeros_like(o_ref)


def index_scores(ik, iq_hm, w_hm, tq, tk):
    b, s, _ = ik.shape
    nq, nk = s // tq, s // tk

    def last_needed(qi):
        return (qi * tq + tq - 1) // tk

    return pl.pallas_call(
        functools.partial(_index_kernel, tq=tq, tk=tk, rc=min(tk, 32)), name="index_scores",
        grid=(b, nq, nk),
        in_specs=[pl.BlockSpec((1, tk, IDX_DIM), lambda bi, qi, kj: (bi, jnp.minimum(kj, last_needed(qi)), 0)),
                  pl.BlockSpec((1, 1, IDX_HEADS * tq, IDX_DIM), lambda bi, qi, kj: (bi, qi, 0, 0)),
                  pl.BlockSpec((1, 1, IDX_HEADS, tq), lambda bi, qi, kj: (bi, qi, 0, 0))],
        out_specs=pl.BlockSpec((1, tk, tq), lambda bi, qi, kj: (bi, kj, qi)),
        out_shape=jax.ShapeDtypeStruct((b, s, s), F32),
        scratch_shapes=[pltpu.VMEM((tk, IDX_HEADS * tq), F32)],
        compiler_params=_params(("parallel", "parallel", "arbitrary")))(ik, iq_hm, w_hm)


def _select_kernel(s_ref, idx_ref, key_ref, c_ref, acc_ref, *, tl, rb, topk, row_stride):
    qi = pl.program_id(1)
    t_lane = qi * tl + lax.broadcasted_iota(I32, (1, tl), 1)
    nblk = ((qi + 1) * tl + rb - 1) // rb
    sub = rb // SUBLANES

    def rows_of(r):
        return pl.ds(pl.multiple_of(r * rb, rb), rb)

    def row_ids(r):
        return r * rb + lax.broadcasted_iota(I32, (rb, tl), 0)

    def fold(m):
        return jnp.sum(m.reshape(sub, SUBLANES, tl), axis=0)

    def make_keys(r, carry):
        x = s_ref[0, rows_of(r), :]
        x = jnp.where(x == 0.0, 0.0, x)
        bits = lax.bitcast_convert_type(x, I32)
        key = bits ^ ((bits >> 31) & 0x7FFFFFFF)
        key_ref[rows_of(r), :] = jnp.where(row_ids(r) <= t_lane, key, INT_MIN)
        return carry

    lax.fori_loop(0, nblk, make_keys, 0)

    def count_ge(cand):
        def body(r, acc):
            return acc + fold((key_ref[rows_of(r), :] >= cand).astype(I32))
        acc = lax.fori_loop(0, nblk, body, jnp.zeros((SUBLANES, tl), I32))
        return jnp.sum(acc, axis=0, keepdims=True)

    zero = jnp.zeros((1, tl), I32)
    thr = jnp.where(count_ge(zero) >= topk, zero, jnp.full((1, tl), INT_MIN, I32))

    def bit_step(i, thr):
        cand = thr | (jnp.int32(1) << (30 - i))
        return jnp.where(count_ge(cand) >= topk, cand, thr)

    thr = lax.fori_loop(0, 31, bit_step, thr)

    def count_gt(r, acc):
        return acc + fold((key_ref[rows_of(r), :] > thr).astype(I32))

    n_gt = jnp.sum(lax.fori_loop(0, nblk, count_gt, jnp.zeros((SUBLANES, tl), I32)), axis=0, keepdims=True)
    need = (topk - n_gt).astype(F32)

    ri = lax.broadcasted_iota(I32, (rb, rb), 0)
    ci = lax.broadcasted_iota(I32, (rb, rb), 1)
    tri = (ci <= ri).astype(BF16)

    def prefix(r, carry):
        carry_eq, carry_c = carry
        key = key_ref[rows_of(r), :]
        eq = jnp.logical_and(key == thr, row_ids(r) <= t_lane)
        ceq = jnp.dot(tri, eq.astype(BF16), preferred_element_type=F32) + carry_eq
        sel = jnp.logical_or(key > thr, jnp.logical_and(eq, ceq <= need))
        c = jnp.dot(tri, sel.astype(BF16), preferred_element_type=F32) + carry_c
        c_ref[rows_of(r), :] = c
        return ceq[rb - 1:rb, :], c[rb - 1:rb, :]

    zf = jnp.zeros((1, tl), F32)
    lax.fori_loop(0, nblk, prefix, (zf, zf))

    group = SUBLANES
    jrow = lax.broadcasted_iota(I32, (topk, tl), 0)
    acc_ref[...] = jnp.zeros_like(acc_ref)

    def block_slots(r, c_before):
        c = c_ref[rows_of(r), :]
        c_last = c[rb - 1:rb, :]
        g_lo = jnp.min(c_before).astype(I32) // group
        g_hi = (jnp.minimum(jnp.max(c_last).astype(I32), topk) + group - 1) // group

        def slot_group(jg, carry):
            counts = [jnp.sum(fold((c <= (jg * group + u).astype(F32)).astype(F32)), axis=0, keepdims=True)
                      for u in range(group)]
            rows = pl.ds(pl.multiple_of(jg * group, group), group)
            acc_ref[rows, :] += jnp.concatenate(counts, axis=0)
            return carry

        lax.fori_loop(g_lo, g_hi, slot_group, 0)
        acc_ref[...] += jnp.where(jrow >= g_hi * group, float(rb), 0.0)
        return c_last

    lax.fori_loop(0, nblk, block_slots, zf)
    idx_ref[0] = jnp.minimum(acc_ref[...].T.astype(I32), s_ref.shape[1] - 1) * row_stride


def select_topk(s_t, topk, row_stride=1):
    b, s, _ = s_t.shape
    tl = _tile(s, LANES)
    rb = _tile(s, 256)
    return pl.pallas_call(
        functools.partial(_select_kernel, tl=tl, rb=rb, topk=topk, row_stride=row_stride), name="select_topk",
        grid=(b, s // tl),
        in_specs=[pl.BlockSpec((1, s, tl), lambda bi, qi: (bi, 0, qi))],
        out_specs=pl.BlockSpec((1, tl, topk), lambda bi, qi: (bi, qi, 0)),
        out_shape=jax.ShapeDtypeStruct((b, s, topk), I32),
        scratch_shapes=[pltpu.VMEM((s, tl), I32), pltpu.VMEM((s, tl), F32), pltpu.VMEM((topk, tl), F32)],
        compiler_params=_params(("parallel", "parallel")))(s_t)


HI16 = -65536
QUERY_BATCH = 16


def _pack_kv_kernel(k_ref, v_ref, o_ref):
    kb = lax.bitcast_convert_type(k_ref[...].astype(BF16).astype(F32), I32)
    vb = lax.bitcast_convert_type(v_ref[...].astype(BF16).astype(F32), I32)
    o_ref[...] = (vb & HI16) | ((kb >> 16) & 0xFFFF)


def pack_kv(k, v):
    n, w = k.shape
    tr = _tile(n, 1024)
    spec = pl.BlockSpec((tr, w), lambda i: (i, 0))
    return pl.pallas_call(
        _pack_kv_kernel, name="pack_kv", grid=(n // tr,), in_specs=[spec, spec], out_specs=spec,
        out_shape=jax.ShapeDtypeStruct((n, w), I32),
        compiler_params=_params(("parallel",)))(k, v)


def _sparse_attn_kernel(idx_ref, q_ref, kv_ref, o_ref, st_ref, *, tq, topk):
    qt = pl.program_id(1)
    scale = HEAD_DIM ** -0.5
    rows_per_iter = SUBLANES * N_KV_HEADS

    def per_query(qq, carry):
        def gather(i, c):
            base = pl.multiple_of(i * SUBLANES, SUBLANES)
            dst = st_ref.at[qq, pl.ds(pl.multiple_of(i * rows_per_iter, rows_per_iter), rows_per_iter)]
            for u in range(SUBLANES):
                r = pl.multiple_of(idx_ref[0, 0, qq * topk + base + u], N_KV_HEADS)
                dst[u * N_KV_HEADS:(u + 1) * N_KV_HEADS, :] = kv_ref[0, pl.ds(r, N_KV_HEADS), :]
            return c

        return lax.fori_loop(0, topk // SUBLANES, gather, carry)

    lax.fori_loop(0, tq, per_query, 0)

    slot = lax.broadcasted_iota(I32, (QUERY_BATCH, KV_GROUP, topk), 2)
    qoff = lax.broadcasted_iota(I32, (QUERY_BATCH, KV_GROUP, topk), 0)

    def attend(qb, carry):
        rows = pl.ds(pl.multiple_of(qb * QUERY_BATCH, QUERY_BATCH), QUERY_BATCH)
        valid = slot < jnp.minimum(qt * tq + qb * QUERY_BATCH + qoff + 1, topk)
        for h in range(N_KV_HEADS):
            x = st_ref[rows, pl.ds(h, topk, stride=N_KV_HEADS), :]
            kb = lax.bitcast_convert_type(x << 16, F32).astype(BF16)
            vb = lax.bitcast_convert_type(x & HI16, F32).astype(BF16)
            s = jnp.einsum('qgd,qjd->qgj', q_ref[0, rows, h], kb, preferred_element_type=F32) * scale
            s = jnp.where(valid, s, -jnp.inf)
            e = jnp.exp(s - jnp.max(s, axis=-1, keepdims=True))
            p = e / jnp.sum(e, axis=-1, keepdims=True)
            o_ref[0, rows, h] = jnp.einsum('qgj,qjd->qgd', p.astype(BF16), vb, preferred_element_type=F32)
        return carry

    lax.fori_loop(0, tq // QUERY_BATCH, attend, 0)


def sparse_attention(idx, q, kv, topk):
    b, s, _ = idx.shape
    tq = _tile(s, 16)
    nqt = s // tq
    idx = idx.reshape(b * nqt, 1, tq * topk)
    return pl.pallas_call(
        functools.partial(_sparse_attn_kernel, tq=tq, topk=topk), name="sparse_attention",
        grid=(b, nqt),
        in_specs=[pl.BlockSpec((1, 1, tq * topk), lambda bi, qt: (bi * nqt + qt, 0, 0), memory_space=pltpu.SMEM),
                  pl.BlockSpec((1, tq, N_KV_HEADS, KV_GROUP, HEAD_DIM), lambda bi, qt: (bi, qt, 0, 0, 0)),
                  pl.BlockSpec((1, s * N_KV_HEADS, HEAD_DIM), lambda bi, qt: (bi, 0, 0),
                               pipeline_mode=pl.Buffered(1))],
        out_specs=pl.BlockSpec((1, tq, N_KV_HEADS, KV_GROUP, HEAD_DIM), lambda bi, qt: (bi, qt, 0, 0, 0)),
        out_shape=jax.ShapeDtypeStruct((b, s, N_KV_HEADS, KV_GROUP, HEAD_DIM), F32),
        scratch_shapes=[pltpu.VMEM((tq, topk * N_KV_HEADS, HEAD_DIM), I32)],
        compiler_params=_params(("parallel", "arbitrary")))(idx, q, kv)


def dsa_mixer(xn, w_in, w_out, positions):
    b, s, d = xn.shape
    n = b * s
    q_cols = d
    kv_cols = N_KV_HEADS * HEAD_DIM
    iq_cols = IDX_HEADS * IDX_DIM
    o0 = 0
    bounds = []
    for width in (q_cols, kv_cols, kv_cols, iq_cols, IDX_DIM, IDX_HEADS):
        bounds.append((o0, o0 + width))
        o0 += width
    w_bf = w_in.astype(BF16)
    x2 = xn.reshape(n, d)
    q, k, v, iq, ik, iw = (matmul(x2, w_bf[:, lo:hi]).reshape(b, s, hi - lo) for lo, hi in bounds)

    inv = ROPE_THETA ** (-jnp.arange(0, ROT_DIM, 2, dtype=F32) / ROT_DIM)
    inv_lane = jnp.tile(inv, LANES // inv.shape[0]).reshape(1, LANES)
    pos = positions.reshape(b, s, 1)
    tq = _tile(s, 256)
    tk = _tile(s, 512)
    q_r = rope(q, pos, inv_lane, d // HEAD_DIM, BF16)
    k_r = rope(k, pos, inv_lane, N_KV_HEADS, F32)
    iq_hm = rope(iq, pos, inv_lane, IDX_HEADS, BF16, head_major_tile=tq)
    ik_r = rope(ik, pos, inv_lane, 1, BF16)
    w_hm = iw.reshape(b, s // tq, tq, IDX_HEADS).transpose(0, 1, 3, 2)

    topk = min(INDEX_TOPK, s // 4)
    s_t = index_scores(ik_r, iq_hm.reshape(b, s // tq, IDX_HEADS * tq, IDX_DIM), w_hm, tq, tk)
    idx = select_topk(s_t, topk, row_stride=N_KV_HEADS)
    kv = pack_kv(k_r.reshape(n, kv_cols), v.reshape(n, kv_cols)).reshape(b, s * N_KV_HEADS, HEAD_DIM)
    o = sparse_attention(idx, q_r.reshape(b, s, N_KV_HEADS, KV_GROUP, HEAD_DIM), kv, topk)
    return matmul(o.reshape(n, d), w_out.astype(BF16))


def memory_xattn(hn, mem_n, wq, wkv, wo, b, s):
    width = wq.shape[1]
    m = mem_n.shape[0] // b
    q = matmul(hn, wq.astype(BF16), out_dtype=BF16).reshape(b, s, width)
    kv = matmul(mem_n, wkv.astype(BF16), out_dtype=BF16)
    km = kv[:, :width].reshape(b, m, width)
    vm = kv[:, width:].reshape(b, m, width)
    o = xattn_core(q, km, vm)
    return matmul(o.reshape(b * s, width), wo.astype(BF16))


def kernel(x, mem, positions, norm_gains, mem_norm, pool_w, pool_scale, dsa_w_in, dsa_w_out,
           xattn_wq, xattn_wkv, xattn_wo, ffn_w_gate_up, ffn_w_down):
    b, s, d = x.shape
    n = b * s
    depth = norm_gains.shape[0]
    mem_n = rms_norm(mem.reshape(-1, d), mem_norm, out_dtype=BF16)
    h = x.reshape(n, d)
    for i in range(depth):
        g = norm_gains[i]
        if i % 2 == 0:
            a = rms_norm(h, g[0])
            a = pool_mixer(a.reshape(b, s, d), pool_w[i // 2], pool_scale[i // 2]).reshape(n, d)
        else:
            a = rms_norm(h, g[0], out_dtype=BF16)
            a = dsa_mixer(a.reshape(b, s, d), dsa_w_in[i // 2], dsa_w_out[i // 2], positions)
        h = rms_norm(a, g[1], res=h)
        c = memory_xattn(rms_norm(h, g[2], out_dtype=BF16), mem_n, xattn_wq[i], xattn_wkv[i], xattn_wo[i], b, s)
        h = rms_norm(c, g[3], res=h)
        act = swiglu_up(rms_norm(h, g[4], out_dtype=BF16), ffn_w_gate_up[i].astype(BF16))
        f = matmul(act, ffn_w_down[i].astype(BF16))
        h = rms_norm(f, g[5], res=h)
    return h.reshape(b, s, d)
```

```python
import functools

import jax
import jax.numpy as jnp
from jax import lax
from jax.experimental import pallas as pl
from jax.experimental.pallas import tpu as pltpu

F32 = jnp.float32
BF16 = jnp.bfloat16
I32 = jnp.int32

LANES = 128
SUBLANES = 8
VMEM_LIMIT_BYTES = 56 * 1024 * 1024

NORM_EPS = 1e-6
POOL_WINDOWS = (2, 4, 8, 16)
MAX_WINDOW = 16
HEAD_DIM = 128
N_KV_HEADS = 4
KV_GROUP = 4
IDX_HEADS = 16
IDX_DIM = 128
INDEX_TOPK = 256
ROPE_THETA = 500000.0
ROT_DIM = 32
XATTN_HEADS = 4
XATTN_DIM = 128
INT_MIN = -(2 ** 31)


def _params(semantics):
    return pltpu.CompilerParams(dimension_semantics=semantics, vmem_limit_bytes=VMEM_LIMIT_BYTES)


def _tile(n, want):
    t = min(n, want)
    while n % t:
        t //= 2
    return t


def _norm_body(x_ref, g_ref):
    x = x_ref[...].astype(F32)
    ms = jnp.mean(x * x, axis=-1, keepdims=True)
    return x * lax.rsqrt(ms + NORM_EPS) * g_ref[...]


def _norm_kernel(x_ref, g_ref, o_ref):
    o_ref[...] = _norm_body(x_ref, g_ref).astype(o_ref.dtype)


def _norm_res_kernel(x_ref, g_ref, r_ref, o_ref):
    o_ref[...] = (r_ref[...] + _norm_body(x_ref, g_ref)).astype(o_ref.dtype)


def rms_norm(x, g, res=None, out_dtype=F32):
    n, d = x.shape
    tr = _tile(n, 256)
    row = pl.BlockSpec((tr, d), lambda i: (i, 0))
    gain = pl.BlockSpec((1, d), lambda i: (0, 0))
    g2 = g.reshape(1, d).astype(F32)
    if res is None:
        kern, specs, args = _norm_kernel, [row, gain], (x, g2)
    else:
        kern, specs, args = _norm_res_kernel, [row, gain, row], (x, g2, res)
    return pl.pallas_call(
        kern, name="rms_norm", grid=(n // tr,), in_specs=specs, out_specs=row,
        out_shape=jax.ShapeDtypeStruct((n, d), out_dtype),
        compiler_params=_params(("parallel",)))(*args)


def _mm_kernel(a_ref, b_ref, o_ref):
    o_ref[...] = jnp.dot(a_ref[...].astype(BF16), b_ref[...].astype(BF16),
                         preferred_element_type=F32).astype(o_ref.dtype)


def matmul(a, b, out_dtype=F32, tm=512, tn=512):
    m, kd = a.shape
    n = b.shape[1]
    tm, tn = _tile(m, tm), _tile(n, tn)
    return pl.pallas_call(
        _mm_kernel, name="matmul",
        grid=(m // tm, n // tn),
        in_specs=[pl.BlockSpec((tm, kd), lambda i, j: (i, 0)),
                  pl.BlockSpec((kd, tn), lambda i, j: (0, j))],
        out_specs=pl.BlockSpec((tm, tn), lambda i, j: (i, j)),
        out_shape=jax.ShapeDtypeStruct((m, n), out_dtype),
        compiler_params=_params(("parallel", "parallel")))(a, b)


def _swiglu_kernel(a_ref, bg_ref, bu_ref, o_ref):
    a = a_ref[...].astype(BF16)
    g = jnp.dot(a, bg_ref[...].astype(BF16), preferred_element_type=F32)
    u = jnp.dot(a, bu_ref[...].astype(BF16), preferred_element_type=F32)
    o_ref[...] = (g * (1.0 / (1.0 + jnp.exp(-g))) * u).astype(o_ref.dtype)


def swiglu_up(a, w_gate_up, tm=512, tn=512):
    m, kd = a.shape
    f = w_gate_up.shape[1] // 2
    tm, tn = _tile(m, tm), _tile(f, tn)
    nf = f // tn
    return pl.pallas_call(
        _swiglu_kernel, name="swiglu_up",
        grid=(m // tm, nf),
        in_specs=[pl.BlockSpec((tm, kd), lambda i, j: (i, 0)),
                  pl.BlockSpec((kd, tn), lambda i, j: (0, j)),
                  pl.BlockSpec((kd, tn), lambda i, j: (0, j + nf))],
        out_specs=pl.BlockSpec((tm, tn), lambda i, j: (i, j)),
        out_shape=jax.ShapeDtypeStruct((m, f), BF16),
        compiler_params=_params(("parallel", "parallel")))(a, w_gate_up, w_gate_up)


def _pool_kernel(x_ref, halo_ref, w_ref, scale_ref, o_ref, pad_ref, *, ts, cg):
    i = pl.program_id(1)
    row = i * ts + lax.broadcasted_iota(I32, (ts, 1), 0)
    for g, win in enumerate(POOL_WINDOWS):
        cols = slice(g * cg, (g + 1) * cg)
        x = x_ref[0, :, cols]
        halo = halo_ref[0, :, cols]
        pad_ref[0:MAX_WINDOW, :] = jnp.where(i == 0, jnp.zeros_like(halo), halo)
        pad_ref[MAX_WINDOW:, :] = x
        wsum = x
        for k in range(1, win):
            wsum = wsum + pad_ref[MAX_WINDOW - k:MAX_WINDOW - k + ts, :]
        cnt = jnp.minimum(row + 1, win).astype(F32)
        p = wsum / cnt - x
        y = jnp.dot(p.astype(BF16), w_ref[g].astype(BF16), preferred_element_type=F32)
        o_ref[0, :, cols] = y * scale_ref[:, cols]


def pool_mixer(xn, w_group, scale):
    b, s, d = xn.shape
    g, cg, _ = w_group.shape
    ts = _tile(s, 512)
    hb = ts // MAX_WINDOW
    return pl.pallas_call(
        functools.partial(_pool_kernel, ts=ts, cg=cg), name="pool_mixer",
        grid=(b, s // ts),
        in_specs=[pl.BlockSpec((1, ts, d), lambda bi, i: (bi, i, 0)),
                  pl.BlockSpec((1, MAX_WINDOW, d), lambda bi, i: (bi, jnp.maximum(i * hb - 1, 0), 0)),
                  pl.BlockSpec((g, cg, cg), lambda bi, i: (0, 0, 0)),
                  pl.BlockSpec((1, d), lambda bi, i: (0, 0))],
        out_specs=pl.BlockSpec((1, ts, d), lambda bi, i: (bi, i, 0)),
        out_shape=jax.ShapeDtypeStruct((b, s, d), F32),
        scratch_shapes=[pltpu.VMEM((MAX_WINDOW + ts, cg), F32)],
        compiler_params=_params(("parallel", "arbitrary")))(xn, xn, w_group.astype(BF16), scale.reshape(1, d))


def _xattn_kernel(q_ref, k_ref, v_ref, o_ref):
    scale = XATTN_DIM ** -0.5
    for h in range(XATTN_HEADS):
        cols = slice(h * XATTN_DIM, (h + 1) * XATTN_DIM)
        q = q_ref[0, :, cols].astype(BF16)
        k = k_ref[0, :, cols].astype(BF16)
        v = v_ref[0, :, cols].astype(BF16)
        s = lax.dot_general(q, k, (((1,), (1,)), ((), ())), preferred_element_type=F32) * scale
        e = jnp.exp(s - jnp.max(s, axis=-1, keepdims=True))
        p = e / jnp.sum(e, axis=-1, keepdims=True)
        o_ref[0, :, cols] = jnp.dot(p.astype(BF16), v, preferred_element_type=F32).astype(o_ref.dtype)


def xattn_core(q, km, vm):
    b, s, w = q.shape
    m = km.shape[1]
    ts = _tile(s, 512)
    return pl.pallas_call(
        _xattn_kernel, name="xattn_core",
        grid=(b, s // ts),
        in_specs=[pl.BlockSpec((1, ts, w), lambda bi, i: (bi, i, 0)),
                  pl.BlockSpec((1, m, w), lambda bi, i: (bi, 0, 0)),
                  pl.BlockSpec((1, m, w), lambda bi, i: (bi, 0, 0))],
        out_specs=pl.BlockSpec((1, ts, w), lambda bi, i: (bi, i, 0)),
        out_shape=jax.ShapeDtypeStruct((b, s, w), BF16),
        compiler_params=_params(("parallel", "parallel")))(q, km, vm)


def _rope_kernel(x_ref, pos_ref, inv_ref, o_ref, *, n_heads, head_major):
    ts = x_ref.shape[1]
    lane = lax.broadcasted_iota(I32, (ts, HEAD_DIM), 1)
    half = ROT_DIM // 2
    ang = pos_ref[0].astype(F32) * inv_ref[...]
    sn = jnp.sin(ang)
    cos_t = jnp.where(lane < ROT_DIM, jnp.cos(ang), 1.0)
    sin_t = jnp.where(lane < half, -sn, jnp.where(lane < ROT_DIM, sn, 0.0))
    for h in range(n_heads):
        x = x_ref[0, :, h * HEAD_DIM:(h + 1) * HEAD_DIM].astype(F32)
        partner = jnp.where(lane < half, pltpu.roll(x, LANES - half, 1), pltpu.roll(x, half, 1))
        y = (x * cos_t + partner * sin_t).astype(o_ref.dtype)
        if head_major:
            o_ref[0, 0, h] = y
        else:
            o_ref[0, :, h * HEAD_DIM:(h + 1) * HEAD_DIM] = y


def rope(x, pos, inv_lane, n_heads, out_dtype, head_major_tile=None):
    b, s, w = x.shape
    ts = head_major_tile or _tile(s, 256)
    if head_major_tile:
        out_shape = jax.ShapeDtypeStruct((b, s // ts, n_heads, ts, HEAD_DIM), out_dtype)
        out_spec = pl.BlockSpec((1, 1, n_heads, ts, HEAD_DIM), lambda bi, i: (bi, i, 0, 0, 0))
    else:
        out_shape = jax.ShapeDtypeStruct(x.shape, out_dtype)
        out_spec = pl.BlockSpec((1, ts, w), lambda bi, i: (bi, i, 0))
    return pl.pallas_call(
        functools.partial(_rope_kernel, n_heads=n_heads, head_major=bool(head_major_tile)), name="rope",
        grid=(b, s // ts),
        in_specs=[pl.BlockSpec((1, ts, w), lambda bi, i: (bi, i, 0)),
                  pl.BlockSpec((1, ts, 1), lambda bi, i: (bi, i, 0)),
                  pl.BlockSpec((1, HEAD_DIM), lambda bi, i: (0, 0))],
        out_specs=out_spec, out_shape=out_shape,
        compiler_params=_params(("parallel", "parallel")))(x, pos, inv_lane)


def _index_kernel(ik_ref, iq_ref, w_ref, o_ref, x_ref, *, tq, tk, rc):
    qi = pl.program_id(1)
    kj = pl.program_id(2)
    needed = kj * tk <= qi * tq + tq - 1

    @pl.when(needed)
    def _():
        x_ref[...] = lax.dot_general(ik_ref[0], iq_ref[0, 0], (((1,), (1,)), ((), ())),
                                     preferred_element_type=F32)
        w = w_ref[0, 0] * (IDX_HEADS ** -0.5 * IDX_DIM ** -0.5)

        def chunk(r, carry):
            rows = pl.ds(pl.multiple_of(r * rc, rc), rc)
            acc = jnp.zeros((rc, tq), F32)
            for h in range(IDX_HEADS):
                acc = acc + jnp.maximum(x_ref[rows, h * tq:(h + 1) * tq], 0.0) * w[h:h + 1, :]
            o_ref[0, rows, :] = acc
            return carry

        lax.fori_loop(0, tk // rc, chunk, 0)

    @pl.when(jnp.logical_not(needed))
    def _():
        o_ref[...] = jnp.zeros_like(o_ref)


def index_scores(ik, iq_hm, w_hm, tq, tk):
    b, s, _ = ik.shape
    nq, nk = s // tq, s // tk

    def last_needed(qi):
        return (qi * tq + tq - 1) // tk

    return pl.pallas_call(
        functools.partial(_index_kernel, tq=tq, tk=tk, rc=min(tk, 32)), name="index_scores",
        grid=(b, nq, nk),
        in_specs=[pl.BlockSpec((1, tk, IDX_DIM), lambda bi, qi, kj: (bi, jnp.minimum(kj, last_needed(qi)), 0)),
                  pl.BlockSpec((1, 1, IDX_HEADS * tq, IDX_DIM), lambda bi, qi, kj: (bi, qi, 0, 0)),
                  pl.BlockSpec((1, 1, IDX_HEADS, tq), lambda bi, qi, kj: (bi, qi, 0, 0))],
        out_specs=pl.BlockSpec((1, tk, tq), lambda bi, qi, kj: (bi, kj, qi)),
        out_shape=jax.ShapeDtypeStruct((b, s, s), F32),
        scratch_shapes=[pltpu.VMEM((tk, IDX_HEADS * tq), F32)],
        compiler_params=_params(("parallel", "parallel", "arbitrary")))(ik, iq_hm, w_hm)


def _select_kernel(s_ref, idx_ref, key_ref, c_ref, acc_ref, *, tl, rb, topk, row_stride):
    qi = pl.program_id(1)
    t_lane = qi * tl + lax.broadcasted_iota(I32, (1, tl), 1)
    nblk = ((qi + 1) * tl + rb - 1) // rb
    sub = rb // SUBLANES

    def rows_of(r):
        return pl.ds(pl.multiple_of(r * rb, rb), rb)

    def row_ids(r):
        return r * rb + lax.broadcasted_iota(I32, (rb, tl), 0)

    def fold(m):
        return jnp.sum(m.reshape(sub, SUBLANES, tl), axis=0)

    def make_keys(r, carry):
        x = s_ref[0, rows_of(r), :]
        x = jnp.where(x == 0.0, 0.0, x)
        bits = lax.bitcast_convert_type(x, I32)
        key = bits ^ ((bits >> 31) & 0x7FFFFFFF)
        key_ref[rows_of(r), :] = jnp.where(row_ids(r) <= t_lane, key, INT_MIN)
        return carry

    lax.fori_loop(0, nblk, make_keys, 0)

    def count_ge(cand):
        def body(r, acc):
            return acc + fold((key_ref[rows_of(r), :] >= cand).astype(I32))
        acc = lax.fori_loop(0, nblk, body, jnp.zeros((SUBLANES, tl), I32))
        return jnp.sum(acc, axis=0, keepdims=True)

    zero = jnp.zeros((1, tl), I32)
    thr = jnp.where(count_ge(zero) >= topk, zero, jnp.full((1, tl), INT_MIN, I32))

    def bit_step(i, thr):
        cand = thr | (jnp.int32(1) << (30 - i))
        return jnp.where(count_ge(cand) >= topk, cand, thr)

    thr = lax.fori_loop(0, 31, bit_step, thr)

    def count_gt(r, acc):
        return acc + fold((key_ref[rows_of(r), :] > thr).astype(I32))

    n_gt = jnp.sum(lax.fori_loop(0, nblk, count_gt, jnp.zeros((SUBLANES, tl), I32)), axis=0, keepdims=True)
    need = (topk - n_gt).astype(F32)

    ri = lax.broadcasted_iota(I32, (rb, rb), 0)
    ci = lax.broadcasted_iota(I32, (rb, rb), 1)
    tri = (ci <= ri).astype(BF16)

    def prefix(r, carry):
        carry_eq, carry_c = carry
        key = key_ref[rows_of(r), :]
        eq = jnp.logical_and(key == thr, row_ids(r) <= t_lane)
        ceq = jnp.dot(tri, eq.astype(BF16), preferred_element_type=F32) + carry_eq
        sel = jnp.logical_or(key > thr, jnp.logical_and(eq, ceq <= need))
        c = jnp.dot(tri, sel.astype(BF16), preferred_element_type=F32) + carry_c
        c_ref[rows_of(r), :] = c
        return ceq[rb - 1:rb, :], c[rb - 1:rb, :]

    zf = jnp.zeros((1, tl), F32)
    lax.fori_loop(0, nblk, prefix, (zf, zf))

    group = SUBLANES
    jrow = lax.broadcasted_iota(I32, (topk, tl), 0)
    acc_ref[...] = jnp.zeros_like(acc_ref)

    def block_slots(r, c_before):
        c = c_ref[rows_of(r), :]
        c_last = c[rb - 1:rb, :]
        g_lo = jnp.min(c_before).astype(I32) // group
        g_hi = (jnp.minimum(jnp.max(c_last).astype(I32), topk) + group - 1) // group

        def slot_group(jg, carry):
            counts = [jnp.sum(fold((c <= (jg * group + u).astype(F32)).astype(F32)), axis=0, keepdims=True)
                      for u in range(group)]
            rows = pl.ds(pl.multiple_of(jg * group, group), group)
            acc_ref[rows, :] += jnp.concatenate(counts, axis=0)
            return carry

        lax.fori_loop(g_lo, g_hi, slot_group, 0)
        acc_ref[...] += jnp.where(jrow >= g_hi * group, float(rb), 0.0)
        return c_last

    lax.fori_loop(0, nblk, block_slots, zf)
    idx_ref[0] = jnp.minimum(acc_ref[...].T.astype(I32), s_ref.shape[1] - 1) * row_stride


def select_topk(s_t, topk, row_stride=1):
    b, s, _ = s_t.shape
    tl = _tile(s, LANES)
    rb = _tile(s, 256)
    return pl.pallas_call(
        functools.partial(_select_kernel, tl=tl, rb=rb, topk=topk, row_stride=row_stride), name="select_topk",
        grid=(b, s // tl),
        in_specs=[pl.BlockSpec((1, s, tl), lambda bi, qi: (bi, 0, qi))],
        out_specs=pl.BlockSpec((1, tl, topk), lambda bi, qi: (bi, qi, 0)),
        out_shape=jax.ShapeDtypeStruct((b, s, topk), I32),
        scratch_shapes=[pltpu.VMEM((s, tl), I32), pltpu.VMEM((s, tl), F32), pltpu.VMEM((topk, tl), F32)],
        compiler_params=_params(("parallel", "parallel")))(s_t)


HI16 = -65536
QUERY_BATCH = 16
GATHER_UNROLL = 64


def _pack_kv_kernel(k_ref, v_ref, o_ref):
    kb = lax.bitcast_convert_type(k_ref[...].astype(BF16).astype(F32), I32)
    vb = lax.bitcast_convert_type(v_ref[...].astype(BF16).astype(F32), I32)
    o_ref[...] = (vb & HI16) | ((kb >> 16) & 0xFFFF)


def pack_kv(k, v):
    n, w = k.shape
    tr = _tile(n, 1024)
    spec = pl.BlockSpec((tr, w), lambda i: (i, 0))
    return pl.pallas_call(
        _pack_kv_kernel, name="pack_kv", grid=(n // tr,), in_specs=[spec, spec], out_specs=spec,
        out_shape=jax.ShapeDtypeStruct((n, w), I32),
        compiler_params=_params(("parallel",)))(k, v)


def _sparse_attn_kernel(idx_ref, q_ref, kv_ref, o_ref, st_ref, *, tq, topk):
    qt = pl.program_id(1)
    scale = HEAD_DIM ** -0.5
    keys_per_iter = min(topk, GATHER_UNROLL)
    rows_per_iter = keys_per_iter * N_KV_HEADS

    def per_query(qq, carry):
        def gather(i, c):
            base = pl.multiple_of(i * keys_per_iter, keys_per_iter)
            dst = st_ref.at[qq, pl.ds(pl.multiple_of(i * rows_per_iter, rows_per_iter), rows_per_iter)]
            for u in range(keys_per_iter):
                r = pl.multiple_of(idx_ref[0, 0, qq * topk + base + u], N_KV_HEADS)
                dst[u * N_KV_HEADS:(u + 1) * N_KV_HEADS, :] = kv_ref[0, pl.ds(r, N_KV_HEADS), :]
            return c

        return lax.fori_loop(0, topk // keys_per_iter, gather, carry)

    lax.fori_loop(0, tq, per_query, 0)

    slot = lax.broadcasted_iota(I32, (QUERY_BATCH, KV_GROUP, topk), 2)
    qoff = lax.broadcasted_iota(I32, (QUERY_BATCH, KV_GROUP, topk), 0)

    def attend(qb, carry):
        rows = pl.ds(pl.multiple_of(qb * QUERY_BATCH, QUERY_BATCH), QUERY_BATCH)
        valid = slot < jnp.minimum(qt * tq + qb * QUERY_BATCH + qoff + 1, topk)
        for h in range(N_KV_HEADS):
            x = st_ref[rows, pl.ds(h, topk, stride=N_KV_HEADS), :]
            kb = lax.bitcast_convert_type(x << 16, F32).astype(BF16)
            vb = lax.bitcast_convert_type(x & HI16, F32).astype(BF16)
            s = jnp.einsum('qgd,qjd->qgj', q_ref[0, rows, h], kb, preferred_element_type=F32) * scale
            s = jnp.where(valid, s, -jnp.inf)
            e = jnp.exp(s - jnp.max(s, axis=-1, keepdims=True))
            p = e / jnp.sum(e, axis=-1, keepdims=True)
            o_ref[0, rows, h] = jnp.einsum('qgj,qjd->qgd', p.astype(BF16), vb, preferred_element_type=F32)
        return carry

    lax.fori_loop(0, tq // QUERY_BATCH, attend, 0)


def sparse_attention(idx, q, kv, topk):
    b, s, _ = idx.shape
    tq = _tile(s, 16)
    nqt = s // tq
    idx = idx.reshape(b * nqt, 1, tq * topk)
    return pl.pallas_call(
        functools.partial(_sparse_attn_kernel, tq=tq, topk=topk), name="sparse_attention",
        grid=(b, nqt),
        in_specs=[pl.BlockSpec((1, 1, tq * topk), lambda bi, qt: (bi * nqt + qt, 0, 0), memory_space=pltpu.SMEM),
                  pl.BlockSpec((1, tq, N_KV_HEADS, KV_GROUP, HEAD_DIM), lambda bi, qt: (bi, qt, 0, 0, 0)),
                  pl.BlockSpec((1, s * N_KV_HEADS, HEAD_DIM), lambda bi, qt: (bi, 0, 0),
                               pipeline_mode=pl.Buffered(1))],
        out_specs=pl.BlockSpec((1, tq, N_KV_HEADS, KV_GROUP, HEAD_DIM), lambda bi, qt: (bi, qt, 0, 0, 0)),
        out_shape=jax.ShapeDtypeStruct((b, s, N_KV_HEADS, KV_GROUP, HEAD_DIM), F32),
        scratch_shapes=[pltpu.VMEM((tq, topk * N_KV_HEADS, HEAD_DIM), I32)],
        compiler_params=_params(("parallel", "arbitrary")))(idx, q, kv)


def dsa_mixer(xn, w_in, w_out, positions):
    b, s, d = xn.shape
    n = b * s
    q_cols = d
    kv_cols = N_KV_HEADS * HEAD_DIM
    iq_cols = IDX_HEADS * IDX_DIM
    o0 = 0
    bounds = []
    for width in (q_cols, kv_cols, kv_cols, iq_cols, IDX_DIM, IDX_HEADS):
        bounds.append((o0, o0 + width))
        o0 += width
    w_bf = w_in.astype(BF16)
    x2 = xn.reshape(n, d)
    q, k, v, iq, ik, iw = (matmul(x2, w_bf[:, lo:hi]).reshape(b, s, hi - lo) for lo, hi in bounds)

    inv = ROPE_THETA ** (-jnp.arange(0, ROT_DIM, 2, dtype=F32) / ROT_DIM)
    inv_lane = jnp.tile(inv, LANES // inv.shape[0]).reshape(1, LANES)
    pos = positions.reshape(b, s, 1)
    tq = _tile(s, 256)
    tk = _tile(s, 512)
    q_r = rope(q, pos, inv_lane, d // HEAD_DIM, BF16)
    k_r = rope(k, pos, inv_lane, N_KV_HEADS, F32)
    iq_hm = rope(iq, pos, inv_lane, IDX_HEADS, BF16, head_major_tile=tq)
    ik_r = rope(ik, pos, inv_lane, 1, BF16)
    w_hm = iw.reshape(b, s // tq, tq, IDX_HEADS).transpose(0, 1, 3, 2)

    topk = min(INDEX_TOPK, s // 4)
    s_t = index_scores(ik_r, iq_hm.reshape(b, s // tq, IDX_HEADS * tq, IDX_DIM), w_hm, tq, tk)
    idx = select_topk(s_t, topk, row_stride=N_KV_HEADS)
    kv = pack_kv(k_r.reshape(n, kv_cols), v.reshape(n, kv_cols)).reshape(b, s * N_KV_HEADS, HEAD_DIM)
    o = sparse_attention(idx, q_r.reshape(b, s, N_KV_HEADS, KV_GROUP, HEAD_DIM), kv, topk)
    return matmul(o.reshape(n, d), w_out.astype(BF16))


def memory_xattn(hn, mem_n, wq, wkv, wo, b, s):
    width = wq.shape[1]
    m = mem_n.shape[0] // b
    q = matmul(hn, wq.astype(BF16), out_dtype=BF16).reshape(b, s, width)
    kv = matmul(mem_n, wkv.astype(BF16), out_dtype=BF16)
    km = kv[:, :width].reshape(b, m, width)
    vm = kv[:, width:].reshape(b, m, width)
    o = xattn_core(q, km, vm)
    return matmul(o.reshape(b * s, width), wo.astype(BF16))


def kernel(x, mem, positions, norm_gains, mem_norm, pool_w, pool_scale, dsa_w_in, dsa_w_out,
           xattn_wq, xattn_wkv, xattn_wo, ffn_w_gate_up, ffn_w_down):
    b, s, d = x.shape
    n = b * s
    depth = norm_gains.shape[0]
    mem_n = rms_norm(mem.reshape(-1, d), mem_norm, out_dtype=BF16)
    h = x.reshape(n, d)
    for i in range(depth):
        g = norm_gains[i]
        if i % 2 == 0:
            a = rms_norm(h, g[0])
            a = pool_mixer(a.reshape(b, s, d), pool_w[i // 2], pool_scale[i // 2]).reshape(n, d)
        else:
            a = rms_norm(h, g[0], out_dtype=BF16)
            a = dsa_mixer(a.reshape(b, s, d), dsa_w_in[i // 2], dsa_w_out[i // 2], positions)
        h = rms_norm(a, g[1], res=h)
        c = memory_xattn(rms_norm(h, g[2], out_dtype=BF16), mem_n, xattn_wq[i], xattn_wkv[i], xattn_wo[i], b, s)
        h = rms_norm(c, g[3], res=h)
        act = swiglu_up(rms_norm(h, g[4], out_dtype=BF16), ffn_w_gate_up[i].astype(BF16))
        f = matmul(act, ffn_w_down[i].astype(BF16))
        h = rms_norm(f, g[5], res=h)
    return h.reshape(b, s, d)
```

```python
import functools

import jax
import jax.numpy as jnp
from jax import lax
from jax.experimental import pallas as pl
from jax.experimental.pallas import tpu as pltpu

F32 = jnp.float32
BF16 = jnp.bfloat16
I32 = jnp.int32

LANES = 128
SUBLANES = 8
VMEM_LIMIT_BYTES = 56 * 1024 * 1024

NORM_EPS = 1e-6
POOL_WINDOWS = (2, 4, 8, 16)
MAX_WINDOW = 16
HEAD_DIM = 128
N_KV_HEADS = 4
KV_GROUP = 4
IDX_HEADS = 16
IDX_DIM = 128
INDEX_TOPK = 256
ROPE_THETA = 500000.0
ROT_DIM = 32
XATTN_HEADS = 4
XATTN_DIM = 128
INT_MIN = -(2 ** 31)
K_UNROLL = 4
SLOT_WINDOW = 64


def _params(semantics):
    return pltpu.CompilerParams(dimension_semantics=semantics, vmem_limit_bytes=VMEM_LIMIT_BYTES)


def _tile(n, want):
    t = min(n, want)
    while n % t:
        t //= 2
    return t


def _norm_body(x_ref, g_ref):
    x = x_ref[...].astype(F32)
    ms = jnp.mean(x * x, axis=-1, keepdims=True)
    return x * lax.rsqrt(ms + NORM_EPS) * g_ref[...]


def _norm_kernel(x_ref, g_ref, o_ref):
    o_ref[...] = _norm_body(x_ref, g_ref).astype(o_ref.dtype)


def _norm_res_kernel(x_ref, g_ref, r_ref, o_ref):
    o_ref[...] = (r_ref[...] + _norm_body(x_ref, g_ref)).astype(o_ref.dtype)


def rms_norm(x, g, res=None, out_dtype=F32):
    n, d = x.shape
    tr = _tile(n, 256)
    row = pl.BlockSpec((tr, d), lambda i: (i, 0))
    gain = pl.BlockSpec((1, d), lambda i: (0, 0))
    g2 = g.reshape(1, d).astype(F32)
    if res is None:
        kern, specs, args = _norm_kernel, [row, gain], (x, g2)
    else:
        kern, specs, args = _norm_res_kernel, [row, gain, row], (x, g2, res)
    return pl.pallas_call(
        kern, name="rms_norm", grid=(n // tr,), in_specs=specs, out_specs=row,
        out_shape=jax.ShapeDtypeStruct((n, d), out_dtype),
        compiler_params=_params(("parallel",)))(*args)


def _mm_kernel(a_ref, b_ref, o_ref):
    o_ref[...] = jnp.dot(a_ref[...].astype(BF16), b_ref[...].astype(BF16),
                         preferred_element_type=F32).astype(o_ref.dtype)


def matmul(a, b, out_dtype=F32, tm=512, tn=512):
    m, kd = a.shape
    n = b.shape[1]
    tm, tn = _tile(m, tm), _tile(n, tn)
    return pl.pallas_call(
        _mm_kernel, name="matmul",
        grid=(m // tm, n // tn),
        in_specs=[pl.BlockSpec((tm, kd), lambda i, j: (i, 0)),
                  pl.BlockSpec((kd, tn), lambda i, j: (0, j))],
        out_specs=pl.BlockSpec((tm, tn), lambda i, j: (i, j)),
        out_shape=jax.ShapeDtypeStruct((m, n), out_dtype),
        compiler_params=_params(("parallel", "parallel")))(a, b)


def _swiglu_kernel(a_ref, bg_ref, bu_ref, o_ref):
    a = a_ref[...].astype(BF16)
    g = jnp.dot(a, bg_ref[...].astype(BF16), preferred_element_type=F32)
    u = jnp.dot(a, bu_ref[...].astype(BF16), preferred_element_type=F32)
    o_ref[...] = (g * (1.0 / (1.0 + jnp.exp(-g))) * u).astype(o_ref.dtype)


def swiglu_up(a, w_gate_up, tm=512, tn=512):
    m, kd = a.shape
    f = w_gate_up.shape[1] // 2
    tm, tn = _tile(m, tm), _tile(f, tn)
    nf = f // tn
    return pl.pallas_call(
        _swiglu_kernel, name="swiglu_up",
        grid=(m // tm, nf),
        in_specs=[pl.BlockSpec((tm, kd), lambda i, j: (i, 0)),
                  pl.BlockSpec((kd, tn), lambda i, j: (0, j)),
                  pl.BlockSpec((kd, tn), lambda i, j: (0, j + nf))],
        out_specs=pl.BlockSpec((tm, tn), lambda i, j: (i, j)),
        out_shape=jax.ShapeDtypeStruct((m, f), BF16),
        compiler_params=_params(("parallel", "parallel")))(a, w_gate_up, w_gate_up)


def _pool_kernel(x_ref, halo_ref, w_ref, scale_ref, o_ref, pad_ref, *, ts, cg):
    i = pl.program_id(1)
    row = i * ts + lax.broadcasted_iota(I32, (ts, 1), 0)
    for g, win in enumerate(POOL_WINDOWS):
        cols = slice(g * cg, (g + 1) * cg)
        x = x_ref[0, :, cols]
        halo = halo_ref[0, :, cols]
        pad_ref[0:MAX_WINDOW, :] = jnp.where(i == 0, jnp.zeros_like(halo), halo)
        pad_ref[MAX_WINDOW:, :] = x
        wsum = x
        for k in range(1, win):
            wsum = wsum + pad_ref[MAX_WINDOW - k:MAX_WINDOW - k + ts, :]
        cnt = jnp.minimum(row + 1, win).astype(F32)
        p = wsum / cnt - x
        y = jnp.dot(p.astype(BF16), w_ref[g].astype(BF16), preferred_element_type=F32)
        o_ref[0, :, cols] = y * scale_ref[:, cols]


def pool_mixer(xn, w_group, scale):
    b, s, d = xn.shape
    g, cg, _ = w_group.shape
    ts = _tile(s, 512)
    hb = ts // MAX_WINDOW
    return pl.pallas_call(
        functools.partial(_pool_kernel, ts=ts, cg=cg), name="pool_mixer",
        grid=(b, s // ts),
        in_specs=[pl.BlockSpec((1, ts, d), lambda bi, i: (bi, i, 0)),
                  pl.BlockSpec((1, MAX_WINDOW, d), lambda bi, i: (bi, jnp.maximum(i * hb - 1, 0), 0)),
                  pl.BlockSpec((g, cg, cg), lambda bi, i: (0, 0, 0)),
                  pl.BlockSpec((1, d), lambda bi, i: (0, 0))],
        out_specs=pl.BlockSpec((1, ts, d), lambda bi, i: (bi, i, 0)),
        out_shape=jax.ShapeDtypeStruct((b, s, d), F32),
        scratch_shapes=[pltpu.VMEM((MAX_WINDOW + ts, cg), F32)],
        compiler_params=_params(("parallel", "arbitrary")))(xn, xn, w_group.astype(BF16), scale.reshape(1, d))


def _xattn_kernel(q_ref, k_ref, v_ref, o_ref):
    scale = XATTN_DIM ** -0.5
    for h in range(XATTN_HEADS):
        cols = slice(h * XATTN_DIM, (h + 1) * XATTN_DIM)
        q = q_ref[0, :, cols].astype(BF16)
        k = k_ref[0, :, cols].astype(BF16)
        v = v_ref[0, :, cols].astype(BF16)
        s = lax.dot_general(q, k, (((1,), (1,)), ((), ())), preferred_element_type=F32) * scale
        e = jnp.exp(s - jnp.max(s, axis=-1, keepdims=True))
        p = e / jnp.sum(e, axis=-1, keepdims=True)
        o_ref[0, :, cols] = jnp.dot(p.astype(BF16), v, preferred_element_type=F32).astype(o_ref.dtype)


def xattn_core(q, km, vm):
    b, s, w = q.shape
    m = km.shape[1]
    ts = _tile(s, 512)
    return pl.pallas_call(
        _xattn_kernel, name="xattn_core",
        grid=(b, s // ts),
        in_specs=[pl.BlockSpec((1, ts, w), lambda bi, i: (bi, i, 0)),
                  pl.BlockSpec((1, m, w), lambda bi, i: (bi, 0, 0)),
                  pl.BlockSpec((1, m, w), lambda bi, i: (bi, 0, 0))],
        out_specs=pl.BlockSpec((1, ts, w), lambda bi, i: (bi, i, 0)),
        out_shape=jax.ShapeDtypeStruct((b, s, w), BF16),
        compiler_params=_params(("parallel", "parallel")))(q, km, vm)


def _rope_kernel(x_ref, pos_ref, inv_ref, o_ref, *, n_heads, head_major):
    ts = x_ref.shape[1]
    lane = lax.broadcasted_iota(I32, (ts, HEAD_DIM), 1)
    half = ROT_DIM // 2
    ang = pos_ref[0].astype(F32) * inv_ref[...]
    sn = jnp.sin(ang)
    cos_t = jnp.where(lane < ROT_DIM, jnp.cos(ang), 1.0)
    sin_t = jnp.where(lane < half, -sn, jnp.where(lane < ROT_DIM, sn, 0.0))
    for h in range(n_heads):
        x = x_ref[0, :, h * HEAD_DIM:(h + 1) * HEAD_DIM].astype(F32)
        partner = jnp.where(lane < half, pltpu.roll(x, LANES - half, 1), pltpu.roll(x, half, 1))
        y = (x * cos_t + partner * sin_t).astype(o_ref.dtype)
        if head_major:
            o_ref[0, 0, h] = y
        else:
            o_ref[0, :, h * HEAD_DIM:(h + 1) * HEAD_DIM] = y


def rope(x, pos, inv_lane, n_heads, out_dtype, head_major_tile=None):
    b, s, w = x.shape
    ts = head_major_tile or _tile(s, 256)
    if head_major_tile:
        out_shape = jax.ShapeDtypeStruct((b, s // ts, n_heads, ts, HEAD_DIM), out_dtype)
        out_spec = pl.BlockSpec((1, 1, n_heads, ts, HEAD_DIM), lambda bi, i: (bi, i, 0, 0, 0))
    else:
        out_shape = jax.ShapeDtypeStruct(x.shape, out_dtype)
        out_spec = pl.BlockSpec((1, ts, w), lambda bi, i: (bi, i, 0))
    return pl.pallas_call(
        functools.partial(_rope_kernel, n_heads=n_heads, head_major=bool(head_major_tile)), name="rope",
        grid=(b, s // ts),
        in_specs=[pl.BlockSpec((1, ts, w), lambda bi, i: (bi, i, 0)),
                  pl.BlockSpec((1, ts, 1), lambda bi, i: (bi, i, 0)),
                  pl.BlockSpec((1, HEAD_DIM), lambda bi, i: (0, 0))],
        out_specs=out_spec, out_shape=out_shape,
        compiler_params=_params(("parallel", "parallel")))(x, pos, inv_lane)


def _index_kernel(ik_ref, iq_ref, w_ref, o_ref, x_ref, *, tq, tk, rc):
    qi = pl.program_id(1)
    kj = pl.program_id(2)
    needed = kj * tk <= qi * tq + tq - 1

    @pl.when(needed)
    def _():
        x_ref[...] = lax.dot_general(ik_ref[0], iq_ref[0, 0], (((1,), (1,)), ((), ())),
                                     preferred_element_type=F32)
        w = w_ref[0, 0] * (IDX_HEADS ** -0.5 * IDX_DIM ** -0.5)

        def chunk(r, carry):
            rows = pl.ds(pl.multiple_of(r * rc, rc), rc)
            acc = jnp.zeros((rc, tq), F32)
            for h in range(IDX_HEADS):
                acc = acc + jnp.maximum(x_ref[rows, h * tq:(h + 1) * tq], 0.0) * w[h:h + 1, :]
            o_ref[0, rows, :] = acc
            return carry

        lax.fori_loop(0, tk // rc, chunk, 0)

    @pl.when(jnp.logical_not(needed))
    def _():
        o_ref[...] = jnp.zeros_like(o_ref)


def index_scores(ik, iq_hm, w_hm, tq, tk):
    b, s, _ = ik.shape
    nq, nk = s // tq, s // tk

    def last_needed(qi):
        return (qi * tq + tq - 1) // tk

    return pl.pallas_call(
        functools.partial(_index_kernel, tq=tq, tk=tk, rc=min(tk, 32)), name="index_scores",
        grid=(b, nq, nk),
        in_specs=[pl.BlockSpec((1, tk, IDX_DIM), lambda bi, qi, kj: (bi, jnp.minimum(kj, last_needed(qi)), 0)),
                  pl.BlockSpec((1, 1, IDX_HEADS * tq, IDX_DIM), lambda bi, qi, kj: (bi, qi, 0, 0)),
                  pl.BlockSpec((1, 1, IDX_HEADS, tq), lambda bi, qi, kj: (bi, qi, 0, 0))],
        out_specs=pl.BlockSpec((1, tk, tq), lambda bi, qi, kj: (bi, kj, qi)),
        out_shape=jax.ShapeDtypeStruct((b, s, s), F32),
        scratch_shapes=[pltpu.VMEM((tk, IDX_HEADS * tq), F32)],
        compiler_params=_params(("parallel", "parallel", "arbitrary")))(ik, iq_hm, w_hm)


def _select_kernel(s_ref, idx_ref, key_ref, c_ref, acc_ref, *, tl, rb, topk, row_stride):
    qi = pl.program_id(1)
    t_lane = qi * tl + lax.broadcasted_iota(I32, (1, tl), 1)
    nblk = ((qi + 1) * tl + rb - 1) // rb
    sub = rb // SUBLANES

    def rows_of(r):
        return pl.ds(pl.multiple_of(r * rb, rb), rb)

    def row_ids(r):
        return r * rb + lax.broadcasted_iota(I32, (rb, tl), 0)

    def fold(m):
        return jnp.sum(m.reshape(sub, SUBLANES, tl), axis=0)

    def make_keys(r, carry):
        x = s_ref[0, rows_of(r), :]
        x = jnp.where(x == 0.0, 0.0, x)
        bits = lax.bitcast_convert_type(x, I32)
        key = bits ^ ((bits >> 31) & 0x7FFFFFFF)
        key_ref[rows_of(r), :] = jnp.where(row_ids(r) <= t_lane, key, INT_MIN)
        return carry

    lax.fori_loop(0, nblk, make_keys, 0)

    def count_ge(cand):
        def body(r, acc):
            return acc + fold((key_ref[rows_of(r), :] >= cand).astype(I32))
        acc = lax.fori_loop(0, nblk, body, jnp.zeros((SUBLANES, tl), I32))
        return jnp.sum(acc, axis=0, keepdims=True)

    zero = jnp.zeros((1, tl), I32)
    thr = jnp.where(count_ge(zero) >= topk, zero, jnp.full((1, tl), INT_MIN, I32))

    def bit_step(i, thr):
        cand = thr | (jnp.int32(1) << (30 - i))
        return jnp.where(count_ge(cand) >= topk, cand, thr)

    thr = lax.fori_loop(0, 31, bit_step, thr)

    def count_gt(r, acc):
        return acc + fold((key_ref[rows_of(r), :] > thr).astype(I32))

    n_gt = jnp.sum(lax.fori_loop(0, nblk, count_gt, jnp.zeros((SUBLANES, tl), I32)), axis=0, keepdims=True)
    need = (topk - n_gt).astype(F32)

    ri = lax.broadcasted_iota(I32, (rb, rb), 0)
    ci = lax.broadcasted_iota(I32, (rb, rb), 1)
    tri = (ci <= ri).astype(BF16)

    def prefix(r, carry_eq):
        key = key_ref[rows_of(r), :]
        eq = jnp.logical_and(key == thr, row_ids(r) <= t_lane)
        ceq = jnp.dot(tri, eq.astype(BF16), preferred_element_type=F32) + carry_eq
        sel = jnp.logical_or(key > thr, jnp.logical_and(eq, ceq <= need))
        c_ref[rows_of(r), :] = jnp.dot(tri, sel.astype(BF16), preferred_element_type=F32)
        return ceq[rb - 1:rb, :]

    zf = jnp.zeros((1, tl), F32)
    lax.fori_loop(0, nblk, prefix, zf)

    window = min(topk, SLOT_WINDOW)
    acc_ref[...] = jnp.zeros_like(acc_ref)

    def block_slots(r, before):
        c = c_ref[rows_of(r), :]
        cnt = c[rb - 1:rb, :]
        before_i = before.astype(I32)
        base_row = (r * rb).astype(F32)
        start = jnp.minimum(jnp.min(before_i) // SUBLANES * SUBLANES, topk - window)
        last = jnp.max((before + cnt).astype(I32))
        kmax = jnp.max(cnt).astype(I32)

        def make_kth(rows, n_rows, first_row):
            jrow = first_row + lax.broadcasted_iota(I32, (n_rows, tl), 0)

            def kth(kq, carry):
                acc = acc_ref[rows, :]
                for u in range(K_UNROLL):
                    k = kq * K_UNROLL + u
                    kf = k.astype(F32)
                    offset = jnp.sum(fold((c <= kf).astype(F32)), axis=0, keepdims=True)
                    hit = jnp.logical_and(jrow == before_i + k, kf < cnt)
                    acc = jnp.where(hit, offset + base_row, acc)
                acc_ref[rows, :] = acc
                return carry

            return kth

        n_iter = (kmax + K_UNROLL - 1) // K_UNROLL

        def in_window():
            rows = pl.ds(pl.multiple_of(start, SUBLANES), window)
            lax.fori_loop(0, n_iter, make_kth(rows, window, start), 0)

        def everywhere():
            lax.fori_loop(0, n_iter, make_kth(slice(None), topk, 0), 0)

        lax.cond(last - start <= window, in_window, everywhere)
        return before + cnt

    lax.fori_loop(0, nblk, block_slots, zf)
    idx_ref[0] = jnp.minimum(acc_ref[...].T.astype(I32), s_ref.shape[1] - 1) * row_stride


def select_topk(s_t, topk, row_stride=1):
    b, s, _ = s_t.shape
    tl = _tile(s, LANES)
    rb = _tile(s, 256)
    return pl.pallas_call(
        functools.partial(_select_kernel, tl=tl, rb=rb, topk=topk, row_stride=row_stride), name="select_topk",
        grid=(b, s // tl),
        in_specs=[pl.BlockSpec((1, s, tl), lambda bi, qi: (bi, 0, qi))],
        out_specs=pl.BlockSpec((1, tl, topk), lambda bi, qi: (bi, qi, 0)),
        out_shape=jax.ShapeDtypeStruct((b, s, topk), I32),
        scratch_shapes=[pltpu.VMEM((s, tl), I32), pltpu.VMEM((s, tl), F32), pltpu.VMEM((topk, tl), F32)],
        compiler_params=_params(("parallel", "parallel")))(s_t)


HI16 = -65536
QUERY_BATCH = 16
GATHER_UNROLL = 64


def _pack_kv_kernel(k_ref, v_ref, o_ref):
    kb = lax.bitcast_convert_type(k_ref[...].astype(BF16).astype(F32), I32)
    vb = lax.bitcast_convert_type(v_ref[...].astype(BF16).astype(F32), I32)
    o_ref[...] = (vb & HI16) | ((kb >> 16) & 0xFFFF)


def pack_kv(k, v):
    n, w = k.shape
    tr = _tile(n, 1024)
    spec = pl.BlockSpec((tr, w), lambda i: (i, 0))
    return pl.pallas_call(
        _pack_kv_kernel, name="pack_kv", grid=(n // tr,), in_specs=[spec, spec], out_specs=spec,
        out_shape=jax.ShapeDtypeStruct((n, w), I32),
        compiler_params=_params(("parallel",)))(k, v)


def _sparse_attn_kernel(idx_ref, q_ref, kv_ref, o_ref, st_ref, *, tq, topk):
    qt = pl.program_id(1)
    scale = HEAD_DIM ** -0.5
    keys_per_iter = min(topk, GATHER_UNROLL)
    rows_per_iter = keys_per_iter * N_KV_HEADS

    def per_query(qq, carry):
        def gather(i, c):
            base = pl.multiple_of(i * keys_per_iter, keys_per_iter)
            dst = st_ref.at[qq, pl.ds(pl.multiple_of(i * rows_per_iter, rows_per_iter), rows_per_iter)]
            for u in range(keys_per_iter):
                r = pl.multiple_of(idx_ref[0, 0, qq * topk + base + u], N_KV_HEADS)
                dst[u * N_KV_HEADS:(u + 1) * N_KV_HEADS, :] = kv_ref[0, pl.ds(r, N_KV_HEADS), :]
            return c

        return lax.fori_loop(0, topk // keys_per_iter, gather, carry)

    lax.fori_loop(0, tq, per_query, 0)

    slot = lax.broadcasted_iota(I32, (QUERY_BATCH, KV_GROUP, topk), 2)
    qoff = lax.broadcasted_iota(I32, (QUERY_BATCH, KV_GROUP, topk), 0)

    def attend(qb, carry):
        rows = pl.ds(pl.multiple_of(qb * QUERY_BATCH, QUERY_BATCH), QUERY_BATCH)
        valid = slot < jnp.minimum(qt * tq + qb * QUERY_BATCH + qoff + 1, topk)
        for h in range(N_KV_HEADS):
            x = st_ref[rows, pl.ds(h, topk, stride=N_KV_HEADS), :]
            kb = lax.bitcast_convert_type(x << 16, F32).astype(BF16)
            vb = lax.bitcast_convert_type(x & HI16, F32).astype(BF16)
            s = jnp.einsum('qgd,qjd->qgj', q_ref[0, rows, h], kb, preferred_element_type=F32) * scale
            s = jnp.where(valid, s, -jnp.inf)
            e = jnp.exp(s - jnp.max(s, axis=-1, keepdims=True))
            p = e / jnp.sum(e, axis=-1, keepdims=True)
            o_ref[0, rows, h] = jnp.einsum('qgj,qjd->qgd', p.astype(BF16), vb, preferred_element_type=F32)
        return carry

    lax.fori_loop(0, tq // QUERY_BATCH, attend, 0)


def sparse_attention(idx, q, kv, topk):
    b, s, _ = idx.shape
    tq = _tile(s, 16)
    nqt = s // tq
    idx = idx.reshape(b * nqt, 1, tq * topk)
    return pl.pallas_call(
        functools.partial(_sparse_attn_kernel, tq=tq, topk=topk), name="sparse_attention",
        grid=(b, nqt),
        in_specs=[pl.BlockSpec((1, 1, tq * topk), lambda bi, qt: (bi * nqt + qt, 0, 0), memory_space=pltpu.SMEM),
                  pl.BlockSpec((1, tq, N_KV_HEADS, KV_GROUP, HEAD_DIM), lambda bi, qt: (bi, qt, 0, 0, 0)),
                  pl.BlockSpec((1, s * N_KV_HEADS, HEAD_DIM), lambda bi, qt: (bi, 0, 0),
                               pipeline_mode=pl.Buffered(1))],
        out_specs=pl.BlockSpec((1, tq, N_KV_HEADS, KV_GROUP, HEAD_DIM), lambda bi, qt: (bi, qt, 0, 0, 0)),
        out_shape=jax.ShapeDtypeStruct((b, s, N_KV_HEADS, KV_GROUP, HEAD_DIM), F32),
        scratch_shapes=[pltpu.VMEM((tq, topk * N_KV_HEADS, HEAD_DIM), I32)],
        compiler_params=_params(("parallel", "arbitrary")))(idx, q, kv)


def dsa_mixer(xn, w_in, w_out, positions):
    b, s, d = xn.shape
    n = b * s
    q_cols = d
    kv_cols = N_KV_HEADS * HEAD_DIM
    iq_cols = IDX_HEADS * IDX_DIM
    o0 = 0
    bounds = []
    for width in (q_cols, kv_cols, kv_cols, iq_cols, IDX_DIM, IDX_HEADS):
        bounds.append((o0, o0 + width))
        o0 += width
    w_bf = w_in.astype(BF16)
    x2 = xn.reshape(n, d)
    q, k, v, iq, ik, iw = (matmul(x2, w_bf[:, lo:hi]).reshape(b, s, hi - lo) for lo, hi in bounds)

    inv = ROPE_THETA ** (-jnp.arange(0, ROT_DIM, 2, dtype=F32) / ROT_DIM)
    inv_lane = jnp.tile(inv, LANES // inv.shape[0]).reshape(1, LANES)
    pos = positions.reshape(b, s, 1)
    tq = _tile(s, 256)
    tk = _tile(s, 512)
    q_r = rope(q, pos, inv_lane, d // HEAD_DIM, BF16)
    k_r = rope(k, pos, inv_lane, N_KV_HEADS, F32)
    iq_hm = rope(iq, pos, inv_lane, IDX_HEADS, BF16, head_major_tile=tq)
    ik_r = rope(ik, pos, inv_lane, 1, BF16)
    w_hm = iw.reshape(b, s // tq, tq, IDX_HEADS).transpose(0, 1, 3, 2)

    topk = min(INDEX_TOPK, s // 4)
    s_t = index_scores(ik_r, iq_hm.reshape(b, s // tq, IDX_HEADS * tq, IDX_DIM), w_hm, tq, tk)
    idx = select_topk(s_t, topk, row_stride=N_KV_HEADS)
    kv = pack_kv(k_r.reshape(n, kv_cols), v.reshape(n, kv_cols)).reshape(b, s * N_KV_HEADS, HEAD_DIM)
    o = sparse_attention(idx, q_r.reshape(b, s, N_KV_HEADS, KV_GROUP, HEAD_DIM), kv, topk)
    return matmul(o.reshape(n, d), w_out.astype(BF16))


def memory_xattn(hn, mem_n, wq, wkv, wo, b, s):
    width = wq.shape[1]
    m = mem_n.shape[0] // b
    q = matmul(hn, wq.astype(BF16), out_dtype=BF16).reshape(b, s, width)
    kv = matmul(mem_n, wkv.astype(BF16), out_dtype=BF16)
    km = kv[:, :width].reshape(b, m, width)
    vm = kv[:, width:].reshape(b, m, width)
    o = xattn_core(q, km, vm)
    return matmul(o.reshape(b * s, width), wo.astype(BF16))


def kernel(x, mem, positions, norm_gains, mem_norm, pool_w, pool_scale, dsa_w_in, dsa_w_out,
           xattn_wq, xattn_wkv, xattn_wo, ffn_w_gate_up, ffn_w_down):
    b, s, d = x.shape
    n = b * s
    depth = norm_gains.shape[0]
    mem_n = rms_norm(mem.reshape(-1, d), mem_norm, out_dtype=BF16)
    h = x.reshape(n, d)
    for i in range(depth):
        g = norm_gains[i]
        if i % 2 == 0:
            a = rms_norm(h, g[0])
            a = pool_mixer(a.reshape(b, s, d), pool_w[i // 2], pool_scale[i // 2]).reshape(n, d)
        else:
            a = rms_norm(h, g[0], out_dtype=BF16)
            a = dsa_mixer(a.reshape(b, s, d), dsa_w_in[i // 2], dsa_w_out[i // 2], positions)
        h = rms_norm(a, g[1], res=h)
        c = memory_xattn(rms_norm(h, g[2], out_dtype=BF16), mem_n, xattn_wq[i], xattn_wkv[i], xattn_wo[i], b, s)
        h = rms_norm(c, g[3], res=h)
        act = swiglu_up(rms_norm(h, g[4], out_dtype=BF16), ffn_w_gate_up[i].astype(BF16))
        f = matmul(act, ffn_w_down[i].astype(BF16))
        h = rms_norm(f, g[5], res=h)
    return h.reshape(b, s, d)
```

```python
import functools

import jax
import jax.numpy as jnp
from jax import lax
from jax.experimental import pallas as pl
from jax.experimental.pallas import tpu as pltpu

F32 = jnp.float32
BF16 = jnp.bfloat16
I32 = jnp.int32

LANES = 128
SUBLANES = 8
VMEM_LIMIT_BYTES = 56 * 1024 * 1024

NORM_EPS = 1e-6
POOL_WINDOWS = (2, 4, 8, 16)
MAX_WINDOW = 16
HEAD_DIM = 128
N_KV_HEADS = 4
KV_GROUP = 4
IDX_HEADS = 16
IDX_DIM = 128
INDEX_TOPK = 256
ROPE_THETA = 500000.0
ROT_DIM = 32
XATTN_HEADS = 4
XATTN_DIM = 128
INT_MIN = -(2 ** 31)
K_UNROLL = 4
SLOT_WINDOW = 64


def _params(semantics):
    return pltpu.CompilerParams(dimension_semantics=semantics, vmem_limit_bytes=VMEM_LIMIT_BYTES)


def _tile(n, want):
    t = min(n, want)
    while n % t:
        t //= 2
    return t


def _norm_body(x_ref, g_ref):
    x = x_ref[...].astype(F32)
    ms = jnp.mean(x * x, axis=-1, keepdims=True)
    return x * lax.rsqrt(ms + NORM_EPS) * g_ref[...]


def _norm_kernel(x_ref, g_ref, o_ref):
    o_ref[...] = _norm_body(x_ref, g_ref).astype(o_ref.dtype)


def _norm_res_kernel(x_ref, g_ref, r_ref, o_ref):
    o_ref[...] = (r_ref[...] + _norm_body(x_ref, g_ref)).astype(o_ref.dtype)


def rms_norm(x, g, res=None, out_dtype=F32):
    n, d = x.shape
    tr = _tile(n, 256)
    row = pl.BlockSpec((tr, d), lambda i: (i, 0))
    gain = pl.BlockSpec((1, d), lambda i: (0, 0))
    g2 = g.reshape(1, d).astype(F32)
    if res is None:
        kern, specs, args = _norm_kernel, [row, gain], (x, g2)
    else:
        kern, specs, args = _norm_res_kernel, [row, gain, row], (x, g2, res)
    return pl.pallas_call(
        kern, name="rms_norm", grid=(n // tr,), in_specs=specs, out_specs=row,
        out_shape=jax.ShapeDtypeStruct((n, d), out_dtype),
        compiler_params=_params(("parallel",)))(*args)


def _mm_kernel(a_ref, b_ref, o_ref):
    o_ref[...] = jnp.dot(a_ref[...].astype(BF16), b_ref[...].astype(BF16),
                         preferred_element_type=F32).astype(o_ref.dtype)


def matmul(a, b, out_dtype=F32, tm=512, tn=512):
    m, kd = a.shape
    n = b.shape[1]
    tm, tn = _tile(m, tm), _tile(n, tn)
    return pl.pallas_call(
        _mm_kernel, name="matmul",
        grid=(m // tm, n // tn),
        in_specs=[pl.BlockSpec((tm, kd), lambda i, j: (i, 0)),
                  pl.BlockSpec((kd, tn), lambda i, j: (0, j))],
        out_specs=pl.BlockSpec((tm, tn), lambda i, j: (i, j)),
        out_shape=jax.ShapeDtypeStruct((m, n), out_dtype),
        compiler_params=_params(("parallel", "parallel")))(a, b)


def _swiglu_kernel(a_ref, bg_ref, bu_ref, o_ref):
    a = a_ref[...].astype(BF16)
    g = jnp.dot(a, bg_ref[...].astype(BF16), preferred_element_type=F32)
    u = jnp.dot(a, bu_ref[...].astype(BF16), preferred_element_type=F32)
    o_ref[...] = (g * (1.0 / (1.0 + jnp.exp(-g))) * u).astype(o_ref.dtype)


def swiglu_up(a, w_gate_up, tm=512, tn=512):
    m, kd = a.shape
    f = w_gate_up.shape[1] // 2
    tm, tn = _tile(m, tm), _tile(f, tn)
    nf = f // tn
    return pl.pallas_call(
        _swiglu_kernel, name="swiglu_up",
        grid=(m // tm, nf),
        in_specs=[pl.BlockSpec((tm, kd), lambda i, j: (i, 0)),
                  pl.BlockSpec((kd, tn), lambda i, j: (0, j)),
                  pl.BlockSpec((kd, tn), lambda i, j: (0, j + nf))],
        out_specs=pl.BlockSpec((tm, tn), lambda i, j: (i, j)),
        out_shape=jax.ShapeDtypeStruct((m, f), BF16),
        compiler_params=_params(("parallel", "parallel")))(a, w_gate_up, w_gate_up)


def _pool_kernel(x_ref, halo_ref, w_ref, scale_ref, o_ref, pad_ref, *, ts, cg):
    i = pl.program_id(1)
    row = i * ts + lax.broadcasted_iota(I32, (ts, 1), 0)
    for g, win in enumerate(POOL_WINDOWS):
        cols = slice(g * cg, (g + 1) * cg)
        x = x_ref[0, :, cols]
        halo = halo_ref[0, :, cols]
        pad_ref[0:MAX_WINDOW, :] = jnp.where(i == 0, jnp.zeros_like(halo), halo)
        pad_ref[MAX_WINDOW:, :] = x
        wsum = x
        for k in range(1, win):
            wsum = wsum + pad_ref[MAX_WINDOW - k:MAX_WINDOW - k + ts, :]
        cnt = jnp.minimum(row + 1, win).astype(F32)
        p = wsum / cnt - x
        y = jnp.dot(p.astype(BF16), w_ref[g].astype(BF16), preferred_element_type=F32)
        o_ref[0, :, cols] = y * scale_ref[:, cols]


def pool_mixer(xn, w_group, scale):
    b, s, d = xn.shape
    g, cg, _ = w_group.shape
    ts = _tile(s, 512)
    hb = ts // MAX_WINDOW
    return pl.pallas_call(
        functools.partial(_pool_kernel, ts=ts, cg=cg), name="pool_mixer",
        grid=(b, s // ts),
        in_specs=[pl.BlockSpec((1, ts, d), lambda bi, i: (bi, i, 0)),
                  pl.BlockSpec((1, MAX_WINDOW, d), lambda bi, i: (bi, jnp.maximum(i * hb - 1, 0), 0)),
                  pl.BlockSpec((g, cg, cg), lambda bi, i: (0, 0, 0)),
                  pl.BlockSpec((1, d), lambda bi, i: (0, 0))],
        out_specs=pl.BlockSpec((1, ts, d), lambda bi, i: (bi, i, 0)),
        out_shape=jax.ShapeDtypeStruct((b, s, d), F32),
        scratch_shapes=[pltpu.VMEM((MAX_WINDOW + ts, cg), F32)],
        compiler_params=_params(("parallel", "arbitrary")))(xn, xn, w_group.astype(BF16), scale.reshape(1, d))


def _xattn_kernel(q_ref, k_ref, v_ref, o_ref):
    scale = XATTN_DIM ** -0.5
    for h in range(XATTN_HEADS):
        cols = slice(h * XATTN_DIM, (h + 1) * XATTN_DIM)
        q = q_ref[0, :, cols].astype(BF16)
        k = k_ref[0, :, cols].astype(BF16)
        v = v_ref[0, :, cols].astype(BF16)
        s = lax.dot_general(q, k, (((1,), (1,)), ((), ())), preferred_element_type=F32) * scale
        e = jnp.exp(s - jnp.max(s, axis=-1, keepdims=True))
        p = e / jnp.sum(e, axis=-1, keepdims=True)
        o_ref[0, :, cols] = jnp.dot(p.astype(BF16), v, preferred_element_type=F32).astype(o_ref.dtype)


def xattn_core(q, km, vm):
    b, s, w = q.shape
    m = km.shape[1]
    ts = _tile(s, 512)
    return pl.pallas_call(
        _xattn_kernel, name="xattn_core",
        grid=(b, s // ts),
        in_specs=[pl.BlockSpec((1, ts, w), lambda bi, i: (bi, i, 0)),
                  pl.BlockSpec((1, m, w), lambda bi, i: (bi, 0, 0)),
                  pl.BlockSpec((1, m, w), lambda bi, i: (bi, 0, 0))],
        out_specs=pl.BlockSpec((1, ts, w), lambda bi, i: (bi, i, 0)),
        out_shape=jax.ShapeDtypeStruct((b, s, w), BF16),
        compiler_params=_params(("parallel", "parallel")))(q, km, vm)


def _rope_kernel(x_ref, pos_ref, inv_ref, o_ref, *, n_heads, head_major):
    ts = x_ref.shape[1]
    lane = lax.broadcasted_iota(I32, (ts, HEAD_DIM), 1)
    half = ROT_DIM // 2
    ang = pos_ref[0].astype(F32) * inv_ref[...]
    sn = jnp.sin(ang)
    cos_t = jnp.where(lane < ROT_DIM, jnp.cos(ang), 1.0)
    sin_t = jnp.where(lane < half, -sn, jnp.where(lane < ROT_DIM, sn, 0.0))
    for h in range(n_heads):
        x = x_ref[0, :, h * HEAD_DIM:(h + 1) * HEAD_DIM].astype(F32)
        partner = jnp.where(lane < half, pltpu.roll(x, LANES - half, 1), pltpu.roll(x, half, 1))
        y = (x * cos_t + partner * sin_t).astype(o_ref.dtype)
        if head_major:
            o_ref[0, 0, h] = y
        else:
            o_ref[0, :, h * HEAD_DIM:(h + 1) * HEAD_DIM] = y


def rope(x, pos, inv_lane, n_heads, out_dtype, head_major_tile=None):
    b, s, w = x.shape
    ts = head_major_tile or _tile(s, 256)
    if head_major_tile:
        out_shape = jax.ShapeDtypeStruct((b, s // ts, n_heads, ts, HEAD_DIM), out_dtype)
        out_spec = pl.BlockSpec((1, 1, n_heads, ts, HEAD_DIM), lambda bi, i: (bi, i, 0, 0, 0))
    else:
        out_shape = jax.ShapeDtypeStruct(x.shape, out_dtype)
        out_spec = pl.BlockSpec((1, ts, w), lambda bi, i: (bi, i, 0))
    return pl.pallas_call(
        functools.partial(_rope_kernel, n_heads=n_heads, head_major=bool(head_major_tile)), name="rope",
        grid=(b, s // ts),
        in_specs=[pl.BlockSpec((1, ts, w), lambda bi, i: (bi, i, 0)),
                  pl.BlockSpec((1, ts, 1), lambda bi, i: (bi, i, 0)),
                  pl.BlockSpec((1, HEAD_DIM), lambda bi, i: (0, 0))],
        out_specs=out_spec, out_shape=out_shape,
        compiler_params=_params(("parallel", "parallel")))(x, pos, inv_lane)


def _index_kernel(ik_ref, iq_ref, w_ref, o_ref, x_ref, *, tq, tk, rc):
    qi = pl.program_id(1)
    kj = pl.program_id(2)
    needed = kj * tk <= qi * tq + tq - 1

    @pl.when(needed)
    def _():
        x_ref[...] = lax.dot_general(ik_ref[0], iq_ref[0, 0], (((1,), (1,)), ((), ())),
                                     preferred_element_type=F32)
        w = w_ref[0, 0] * (IDX_HEADS ** -0.5 * IDX_DIM ** -0.5)

        def chunk(r, carry):
            rows = pl.ds(pl.multiple_of(r * rc, rc), rc)
            acc = jnp.zeros((rc, tq), F32)
            for h in range(IDX_HEADS):
                acc = acc + jnp.maximum(x_ref[rows, h * tq:(h + 1) * tq], 0.0) * w[h:h + 1, :]
            o_ref[0, rows, :] = acc
            return carry

        lax.fori_loop(0, tk // rc, chunk, 0)

    @pl.when(jnp.logical_not(needed))
    def _():
        o_ref[...] = jnp.zeros_like(o_ref)


def index_scores(ik, iq_hm, w_hm, tq, tk):
    b, s, _ = ik.shape
    nq, nk = s // tq, s // tk

    def last_needed(qi):
        return (qi * tq + tq - 1) // tk

    return pl.pallas_call(
        functools.partial(_index_kernel, tq=tq, tk=tk, rc=min(tk, 32)), name="index_scores",
        grid=(b, nq, nk),
        in_specs=[pl.BlockSpec((1, tk, IDX_DIM), lambda bi, qi, kj: (bi, jnp.minimum(kj, last_needed(qi)), 0)),
                  pl.BlockSpec((1, 1, IDX_HEADS * tq, IDX_DIM), lambda bi, qi, kj: (bi, qi, 0, 0)),
                  pl.BlockSpec((1, 1, IDX_HEADS, tq), lambda bi, qi, kj: (bi, qi, 0, 0))],
        out_specs=pl.BlockSpec((1, tk, tq), lambda bi, qi, kj: (bi, kj, qi)),
        out_shape=jax.ShapeDtypeStruct((b, s, s), F32),
        scratch_shapes=[pltpu.VMEM((tk, IDX_HEADS * tq), F32)],
        compiler_params=_params(("parallel", "parallel", "arbitrary")))(ik, iq_hm, w_hm)


def _select_kernel(s_ref, idx_ref, key_ref, c_ref, acc_ref, *, tl, rb, pair, topk, row_stride):
    qi = pl.program_id(1)
    t_lane = qi * tl + lax.broadcasted_iota(I32, (1, tl), 1)
    nblk = ((qi + 1) * tl + rb - 1) // rb
    sub = rb // SUBLANES

    def rows_of(r):
        return pl.ds(pl.multiple_of(r * rb, rb), rb)

    def row_ids(r):
        return r * rb + lax.broadcasted_iota(I32, (rb, tl), 0)

    def fold(m):
        return jnp.sum(m.reshape(sub, SUBLANES, tl), axis=0)

    def make_keys(r, carry):
        x = s_ref[0, rows_of(r), :]
        x = jnp.where(x == 0.0, 0.0, x)
        bits = lax.bitcast_convert_type(x, I32)
        key = bits ^ ((bits >> 31) & 0x7FFFFFFF)
        key_ref[rows_of(r), :] = jnp.where(row_ids(r) <= t_lane, key, INT_MIN)
        return carry

    ngrp = (nblk + pair - 1) // pair
    lax.fori_loop(0, ngrp * pair, make_keys, 0)

    def count_rows(pred):
        def body(g, acc):
            for u in range(pair):
                acc = acc + fold(pred(key_ref[rows_of(g * pair + u), :]).astype(I32))
            return acc
        acc = lax.fori_loop(0, ngrp, body, jnp.zeros((SUBLANES, tl), I32))
        return jnp.sum(acc, axis=0, keepdims=True)

    zero = jnp.zeros((1, tl), I32)
    thr = jnp.where(count_rows(lambda k: k >= zero) >= topk, zero, jnp.full((1, tl), INT_MIN, I32))

    def bit_step(i, thr):
        cand = thr | (jnp.int32(1) << (30 - i))
        return jnp.where(count_rows(lambda k: k >= cand) >= topk, cand, thr)

    thr = lax.fori_loop(0, 31, bit_step, thr)
    need = (topk - count_rows(lambda k: k > thr)).astype(F32)

    ri = lax.broadcasted_iota(I32, (rb, rb), 0)
    ci = lax.broadcasted_iota(I32, (rb, rb), 1)
    tri = (ci <= ri).astype(BF16)

    def prefix(g, carry_eq):
        for u in range(pair):
            r = g * pair + u
            key = key_ref[rows_of(r), :]
            eq = jnp.logical_and(key == thr, row_ids(r) <= t_lane)
            ceq = jnp.dot(tri, eq.astype(BF16), preferred_element_type=F32) + carry_eq
            sel = jnp.logical_or(key > thr, jnp.logical_and(eq, ceq <= need))
            c_ref[rows_of(r), :] = jnp.dot(tri, sel.astype(BF16), preferred_element_type=F32)
            carry_eq = carry_eq + jnp.sum(fold(eq.astype(F32)), axis=0, keepdims=True)
        return carry_eq

    zf = jnp.zeros((1, tl), F32)
    lax.fori_loop(0, ngrp, prefix, zf)

    window = min(topk, SLOT_WINDOW)
    acc_ref[...] = jnp.zeros_like(acc_ref)

    def block_slots(r, before):
        c = c_ref[rows_of(r), :]
        cnt = c[rb - 1:rb, :]
        before_i = before.astype(I32)
        base_row = (r * rb).astype(F32)
        start = jnp.minimum(jnp.min(before_i) // SUBLANES * SUBLANES, topk - window)
        last = jnp.max((before + cnt).astype(I32))
        kmax = jnp.max(cnt).astype(I32)

        def make_kth(rows, n_rows, first_row):
            jrow = first_row + lax.broadcasted_iota(I32, (n_rows, tl), 0)

            def kth(kq, carry):
                acc = acc_ref[rows, :]
                for u in range(K_UNROLL):
                    k = kq * K_UNROLL + u
                    kf = k.astype(F32)
                    offset = jnp.sum(fold((c <= kf).astype(F32)), axis=0, keepdims=True)
                    hit = jnp.logical_and(jrow == before_i + k, kf < cnt)
                    acc = jnp.where(hit, offset + base_row, acc)
                acc_ref[rows, :] = acc
                return carry

            return kth

        n_iter = (kmax + K_UNROLL - 1) // K_UNROLL

        def in_window():
            rows = pl.ds(pl.multiple_of(start, SUBLANES), window)
            lax.fori_loop(0, n_iter, make_kth(rows, window, start), 0)

        def everywhere():
            lax.fori_loop(0, n_iter, make_kth(slice(None), topk, 0), 0)

        lax.cond(last - start <= window, in_window, everywhere)
        return before + cnt

    lax.fori_loop(0, nblk, block_slots, zf)
    idx_ref[0] = jnp.minimum(acc_ref[...].T.astype(I32), s_ref.shape[1] - 1) * row_stride


def select_topk(s_t, topk, row_stride=1):
    b, s, _ = s_t.shape
    tl = _tile(s, LANES)
    rb = _tile(s, 256)
    pair = 2 if (s // rb) % 2 == 0 else 1
    return pl.pallas_call(
        functools.partial(_select_kernel, tl=tl, rb=rb, pair=pair, topk=topk, row_stride=row_stride),
        name="select_topk",
        grid=(b, s // tl),
        in_specs=[pl.BlockSpec((1, s, tl), lambda bi, qi: (bi, 0, qi))],
        out_specs=pl.BlockSpec((1, tl, topk), lambda bi, qi: (bi, qi, 0)),
        out_shape=jax.ShapeDtypeStruct((b, s, topk), I32),
        scratch_shapes=[pltpu.VMEM((s, tl), I32), pltpu.VMEM((s, tl), F32), pltpu.VMEM((topk, tl), F32)],
        compiler_params=_params(("parallel", "parallel")))(s_t)


HI16 = -65536
QUERY_BATCH = 16
GATHER_UNROLL = 64


def _pack_kv_kernel(k_ref, v_ref, o_ref):
    kb = lax.bitcast_convert_type(k_ref[...].astype(BF16).astype(F32), I32)
    vb = lax.bitcast_convert_type(v_ref[...].astype(BF16).astype(F32), I32)
    o_ref[...] = (vb & HI16) | ((kb >> 16) & 0xFFFF)


def pack_kv(k, v):
    n, w = k.shape
    tr = _tile(n, 1024)
    spec = pl.BlockSpec((tr, w), lambda i: (i, 0))
    return pl.pallas_call(
        _pack_kv_kernel, name="pack_kv", grid=(n // tr,), in_specs=[spec, spec], out_specs=spec,
        out_shape=jax.ShapeDtypeStruct((n, w), I32),
        compiler_params=_params(("parallel",)))(k, v)


def _sparse_attn_kernel(idx_ref, q_ref, kv_ref, o_ref, st_ref, *, tq, topk):
    qt = pl.program_id(1)
    scale = HEAD_DIM ** -0.5
    keys_per_iter = min(topk, GATHER_UNROLL)
    rows_per_iter = keys_per_iter * N_KV_HEADS

    def per_query(qq, carry):
        def gather(i, c):
            base = pl.multiple_of(i * keys_per_iter, keys_per_iter)
            dst = st_ref.at[qq, pl.ds(pl.multiple_of(i * rows_per_iter, rows_per_iter), rows_per_iter)]
            for u in range(keys_per_iter):
                r = pl.multiple_of(idx_ref[0, 0, qq * topk + base + u], N_KV_HEADS)
                dst[u * N_KV_HEADS:(u + 1) * N_KV_HEADS, :] = kv_ref[0, pl.ds(r, N_KV_HEADS), :]
            return c

        return lax.fori_loop(0, topk // keys_per_iter, gather, carry)

    lax.fori_loop(0, tq, per_query, 0)

    slot = lax.broadcasted_iota(I32, (QUERY_BATCH, KV_GROUP, topk), 2)
    qoff = lax.broadcasted_iota(I32, (QUERY_BATCH, KV_GROUP, topk), 0)

    def attend(qb, carry):
        rows = pl.ds(pl.multiple_of(qb * QUERY_BATCH, QUERY_BATCH), QUERY_BATCH)
        valid = slot < jnp.minimum(qt * tq + qb * QUERY_BATCH + qoff + 1, topk)
        for h in range(N_KV_HEADS):
            x = st_ref[rows, pl.ds(h, topk, stride=N_KV_HEADS), :]
            kb = lax.bitcast_convert_type(x << 16, F32).astype(BF16)
            vb = lax.bitcast_convert_type(x & HI16, F32).astype(BF16)
            s = jnp.einsum('qgd,qjd->qgj', q_ref[0, rows, h], kb, preferred_element_type=F32) * scale
            s = jnp.where(valid, s, -jnp.inf)
            e = jnp.exp(s - jnp.max(s, axis=-1, keepdims=True))
            p = e / jnp.sum(e, axis=-1, keepdims=True)
            o_ref[0, rows, h] = jnp.einsum('qgj,qjd->qgd', p.astype(BF16), vb, preferred_element_type=F32)
        return carry

    lax.fori_loop(0, tq // QUERY_BATCH, attend, 0)


def sparse_attention(idx, q, kv, topk):
    b, s, _ = idx.shape
    tq = _tile(s, 16)
    nqt = s // tq
    idx = idx.reshape(b * nqt, 1, tq * topk)
    return pl.pallas_call(
        functools.partial(_sparse_attn_kernel, tq=tq, topk=topk), name="sparse_attention",
        grid=(b, nqt),
        in_specs=[pl.BlockSpec((1, 1, tq * topk), lambda bi, qt: (bi * nqt + qt, 0, 0), memory_space=pltpu.SMEM),
                  pl.BlockSpec((1, tq, N_KV_HEADS, KV_GROUP, HEAD_DIM), lambda bi, qt: (bi, qt, 0, 0, 0)),
                  pl.BlockSpec((1, s * N_KV_HEADS, HEAD_DIM), lambda bi, qt: (bi, 0, 0),
                               pipeline_mode=pl.Buffered(1))],
        out_specs=pl.BlockSpec((1, tq, N_KV_HEADS, KV_GROUP, HEAD_DIM), lambda bi, qt: (bi, qt, 0, 0, 0)),
        out_shape=jax.ShapeDtypeStruct((b, s, N_KV_HEADS, KV_GROUP, HEAD_DIM), F32),
        scratch_shapes=[pltpu.VMEM((tq, topk * N_KV_HEADS, HEAD_DIM), I32)],
        compiler_params=_params(("parallel", "arbitrary")))(idx, q, kv)


def dsa_mixer(xn, w_in, w_out, positions):
    b, s, d = xn.shape
    n = b * s
    q_cols = d
    kv_cols = N_KV_HEADS * HEAD_DIM
    iq_cols = IDX_HEADS * IDX_DIM
    o0 = 0
    bounds = []
    for width in (q_cols, kv_cols, kv_cols, iq_cols, IDX_DIM, IDX_HEADS):
        bounds.append((o0, o0 + width))
        o0 += width
    w_bf = w_in.astype(BF16)
    x2 = xn.reshape(n, d)
    q, k, v, iq, ik, iw = (matmul(x2, w_bf[:, lo:hi]).reshape(b, s, hi - lo) for lo, hi in bounds)

    inv = ROPE_THETA ** (-jnp.arange(0, ROT_DIM, 2, dtype=F32) / ROT_DIM)
    inv_lane = jnp.tile(inv, LANES // inv.shape[0]).reshape(1, LANES)
    pos = positions.reshape(b, s, 1)
    tq = _tile(s, 256)
    tk = _tile(s, 512)
    q_r = rope(q, pos, inv_lane, d // HEAD_DIM, BF16)
    k_r = rope(k, pos, inv_lane, N_KV_HEADS, F32)
    iq_hm = rope(iq, pos, inv_lane, IDX_HEADS, BF16, head_major_tile=tq)
    ik_r = rope(ik, pos, inv_lane, 1, BF16)
    w_hm = iw.reshape(b, s // tq, tq, IDX_HEADS).transpose(0, 1, 3, 2)

    topk = min(INDEX_TOPK, s // 4)
    s_t = index_scores(ik_r, iq_hm.reshape(b, s // tq, IDX_HEADS * tq, IDX_DIM), w_hm, tq, tk)
    idx = select_topk(s_t, topk, row_stride=N_KV_HEADS)
    kv = pack_kv(k_r.reshape(n, kv_cols), v.reshape(n, kv_cols)).reshape(b, s * N_KV_HEADS, HEAD_DIM)
    o = sparse_attention(idx, q_r.reshape(b, s, N_KV_HEADS, KV_GROUP, HEAD_DIM), kv, topk)
    return matmul(o.reshape(n, d), w_out.astype(BF16))


def memory_xattn(hn, mem_n, wq, wkv, wo, b, s):
    width = wq.shape[1]
    m = mem_n.shape[0] // b
    q = matmul(hn, wq.astype(BF16), out_dtype=BF16).reshape(b, s, width)
    kv = matmul(mem_n, wkv.astype(BF16), out_dtype=BF16)
    km = kv[:, :width].reshape(b, m, width)
    vm = kv[:, width:].reshape(b, m, width)
    o = xattn_core(q, km, vm)
    return matmul(o.reshape(b * s, width), wo.astype(BF16))


def kernel(x, mem, positions, norm_gains, mem_norm, pool_w, pool_scale, dsa_w_in, dsa_w_out,
           xattn_wq, xattn_wkv, xattn_wo, ffn_w_gate_up, ffn_w_down):
    b, s, d = x.shape
    n = b * s
    depth = norm_gains.shape[0]
    mem_n = rms_norm(mem.reshape(-1, d), mem_norm, out_dtype=BF16)
    h = x.reshape(n, d)
    for i in range(depth):
        g = norm_gains[i]
        if i % 2 == 0:
            a = rms_norm(h, g[0])
            a = pool_mixer(a.reshape(b, s, d), pool_w[i // 2], pool_scale[i // 2]).reshape(n, d)
        else:
            a = rms_norm(h, g[0], out_dtype=BF16)
            a = dsa_mixer(a.reshape(b, s, d), dsa_w_in[i // 2], dsa_w_out[i // 2], positions)
        h = rms_norm(a, g[1], res=h)
        c = memory_xattn(rms_norm(h, g[2], out_dtype=BF16), mem_n, xattn_wq[i], xattn_wkv[i], xattn_wo[i], b, s)
        h = rms_norm(c, g[3], res=h)
        act = swiglu_up(rms_norm(h, g[4], out_dtype=BF16), ffn_w_gate_up[i].astype(BF16))
        f = matmul(act, ffn_w_down[i].astype(BF16))
        h = rms_norm(f, g[5], res=h)
    return h.reshape(b, s, d)
```

```python
import functools

import jax
import jax.numpy as jnp
from jax import lax
from jax.experimental import pallas as pl
from jax.experimental.pallas import tpu as pltpu

F32 = jnp.float32
BF16 = jnp.bfloat16
I32 = jnp.int32

LANES = 128
SUBLANES = 8
VMEM_LIMIT_BYTES = 56 * 1024 * 1024

NORM_EPS = 1e-6
POOL_WINDOWS = (2, 4, 8, 16)
MAX_WINDOW = 16
HEAD_DIM = 128
N_KV_HEADS = 4
KV_GROUP = 4
IDX_HEADS = 16
IDX_DIM = 128
INDEX_TOPK = 256
ROPE_THETA = 500000.0
ROT_DIM = 32
XATTN_HEADS = 4
XATTN_DIM = 128
INT_MIN = -(2 ** 31)
K_UNROLL = 4
SLOT_WINDOW = 64


def _params(semantics):
    return pltpu.CompilerParams(dimension_semantics=semantics, vmem_limit_bytes=VMEM_LIMIT_BYTES)


def _tile(n, want):
    t = min(n, want)
    while n % t:
        t //= 2
    return t


def _norm_body(x_ref, g_ref):
    x = x_ref[...].astype(F32)
    ms = jnp.mean(x * x, axis=-1, keepdims=True)
    return x * lax.rsqrt(ms + NORM_EPS) * g_ref[...]


def _norm_kernel(x_ref, g_ref, o_ref):
    o_ref[...] = _norm_body(x_ref, g_ref).astype(o_ref.dtype)


def _norm_res_kernel(x_ref, g_ref, r_ref, o_ref):
    o_ref[...] = (r_ref[...] + _norm_body(x_ref, g_ref)).astype(o_ref.dtype)


def rms_norm(x, g, res=None, out_dtype=F32):
    n, d = x.shape
    tr = _tile(n, 256)
    row = pl.BlockSpec((tr, d), lambda i: (i, 0))
    gain = pl.BlockSpec((1, d), lambda i: (0, 0))
    g2 = g.reshape(1, d).astype(F32)
    if res is None:
        kern, specs, args = _norm_kernel, [row, gain], (x, g2)
    else:
        kern, specs, args = _norm_res_kernel, [row, gain, row], (x, g2, res)
    return pl.pallas_call(
        kern, name="rms_norm", grid=(n // tr,), in_specs=specs, out_specs=row,
        out_shape=jax.ShapeDtypeStruct((n, d), out_dtype),
        compiler_params=_params(("parallel",)))(*args)


def _mm_kernel(a_ref, b_ref, o_ref):
    o_ref[...] = jnp.dot(a_ref[...].astype(BF16), b_ref[...].astype(BF16),
                         preferred_element_type=F32).astype(o_ref.dtype)


def matmul(a, b, out_dtype=F32, tm=512, tn=512):
    m, kd = a.shape
    n = b.shape[1]
    tm, tn = _tile(m, tm), _tile(n, tn)
    return pl.pallas_call(
        _mm_kernel, name="matmul",
        grid=(m // tm, n // tn),
        in_specs=[pl.BlockSpec((tm, kd), lambda i, j: (i, 0)),
                  pl.BlockSpec((kd, tn), lambda i, j: (0, j))],
        out_specs=pl.BlockSpec((tm, tn), lambda i, j: (i, j)),
        out_shape=jax.ShapeDtypeStruct((m, n), out_dtype),
        compiler_params=_params(("parallel", "parallel")))(a, b)


def _swiglu_kernel(a_ref, bg_ref, bu_ref, o_ref):
    a = a_ref[...].astype(BF16)
    g = jnp.dot(a, bg_ref[...].astype(BF16), preferred_element_type=F32)
    u = jnp.dot(a, bu_ref[...].astype(BF16), preferred_element_type=F32)
    o_ref[...] = (g * (1.0 / (1.0 + jnp.exp(-g))) * u).astype(o_ref.dtype)


def swiglu_up(a, w_gate_up, tm=512, tn=512):
    m, kd = a.shape
    f = w_gate_up.shape[1] // 2
    tm, tn = _tile(m, tm), _tile(f, tn)
    nf = f // tn
    return pl.pallas_call(
        _swiglu_kernel, name="swiglu_up",
        grid=(m // tm, nf),
        in_specs=[pl.BlockSpec((tm, kd), lambda i, j: (i, 0)),
                  pl.BlockSpec((kd, tn), lambda i, j: (0, j)),
                  pl.BlockSpec((kd, tn), lambda i, j: (0, j + nf))],
        out_specs=pl.BlockSpec((tm, tn), lambda i, j: (i, j)),
        out_shape=jax.ShapeDtypeStruct((m, f), BF16),
        compiler_params=_params(("parallel", "parallel")))(a, w_gate_up, w_gate_up)


def _pool_kernel(x_ref, halo_ref, w_ref, scale_ref, o_ref, pad_ref, *, ts, cg):
    i = pl.program_id(1)
    row = i * ts + lax.broadcasted_iota(I32, (ts, 1), 0)
    for g, win in enumerate(POOL_WINDOWS):
        cols = slice(g * cg, (g + 1) * cg)
        x = x_ref[0, :, cols]
        halo = halo_ref[0, :, cols]
        pad_ref[0:MAX_WINDOW, :] = jnp.where(i == 0, jnp.zeros_like(halo), halo)
        pad_ref[MAX_WINDOW:, :] = x
        wsum = x
        for k in range(1, win):
            wsum = wsum + pad_ref[MAX_WINDOW - k:MAX_WINDOW - k + ts, :]
        cnt = jnp.minimum(row + 1, win).astype(F32)
        p = wsum / cnt - x
        y = jnp.dot(p.astype(BF16), w_ref[g].astype(BF16), preferred_element_type=F32)
        o_ref[0, :, cols] = y * scale_ref[:, cols]


def pool_mixer(xn, w_group, scale):
    b, s, d = xn.shape
    g, cg, _ = w_group.shape
    ts = _tile(s, 512)
    hb = ts // MAX_WINDOW
    return pl.pallas_call(
        functools.partial(_pool_kernel, ts=ts, cg=cg), name="pool_mixer",
        grid=(b, s // ts),
        in_specs=[pl.BlockSpec((1, ts, d), lambda bi, i: (bi, i, 0)),
                  pl.BlockSpec((1, MAX_WINDOW, d), lambda bi, i: (bi, jnp.maximum(i * hb - 1, 0), 0)),
                  pl.BlockSpec((g, cg, cg), lambda bi, i: (0, 0, 0)),
                  pl.BlockSpec((1, d), lambda bi, i: (0, 0))],
        out_specs=pl.BlockSpec((1, ts, d), lambda bi, i: (bi, i, 0)),
        out_shape=jax.ShapeDtypeStruct((b, s, d), F32),
        scratch_shapes=[pltpu.VMEM((MAX_WINDOW + ts, cg), F32)],
        compiler_params=_params(("parallel", "arbitrary")))(xn, xn, w_group.astype(BF16), scale.reshape(1, d))


def _xattn_kernel(q_ref, k_ref, v_ref, o_ref):
    scale = XATTN_DIM ** -0.5
    for h in range(XATTN_HEADS):
        cols = slice(h * XATTN_DIM, (h + 1) * XATTN_DIM)
        q = q_ref[0, :, cols].astype(BF16)
        k = k_ref[0, :, cols].astype(BF16)
        v = v_ref[0, :, cols].astype(BF16)
        s = lax.dot_general(q, k, (((1,), (1,)), ((), ())), preferred_element_type=F32) * scale
        e = jnp.exp(s - jnp.max(s, axis=-1, keepdims=True))
        p = e / jnp.sum(e, axis=-1, keepdims=True)
        o_ref[0, :, cols] = jnp.dot(p.astype(BF16), v, preferred_element_type=F32).astype(o_ref.dtype)


def xattn_core(q, km, vm):
    b, s, w = q.shape
    m = km.shape[1]
    ts = _tile(s, 512)
    return pl.pallas_call(
        _xattn_kernel, name="xattn_core",
        grid=(b, s // ts),
        in_specs=[pl.BlockSpec((1, ts, w), lambda bi, i: (bi, i, 0)),
                  pl.BlockSpec((1, m, w), lambda bi, i: (bi, 0, 0)),
                  pl.BlockSpec((1, m, w), lambda bi, i: (bi, 0, 0))],
        out_specs=pl.BlockSpec((1, ts, w), lambda bi, i: (bi, i, 0)),
        out_shape=jax.ShapeDtypeStruct((b, s, w), BF16),
        compiler_params=_params(("parallel", "parallel")))(q, km, vm)


def _rope_kernel(x_ref, pos_ref, inv_ref, o_ref, *, n_heads, head_major):
    ts = x_ref.shape[1]
    lane = lax.broadcasted_iota(I32, (ts, HEAD_DIM), 1)
    half = ROT_DIM // 2
    ang = pos_ref[0].astype(F32) * inv_ref[...]
    sn = jnp.sin(ang)
    cos_t = jnp.where(lane < ROT_DIM, jnp.cos(ang), 1.0)
    sin_t = jnp.where(lane < half, -sn, jnp.where(lane < ROT_DIM, sn, 0.0))
    for h in range(n_heads):
        x = x_ref[0, :, h * HEAD_DIM:(h + 1) * HEAD_DIM].astype(F32)
        partner = jnp.where(lane < half, pltpu.roll(x, LANES - half, 1), pltpu.roll(x, half, 1))
        y = (x * cos_t + partner * sin_t).astype(o_ref.dtype)
        if head_major:
            o_ref[0, 0, h] = y
        else:
            o_ref[0, :, h * HEAD_DIM:(h + 1) * HEAD_DIM] = y


def rope(x, pos, inv_lane, n_heads, out_dtype, head_major_tile=None):
    b, s, w = x.shape
    ts = head_major_tile or _tile(s, 256)
    if head_major_tile:
        out_shape = jax.ShapeDtypeStruct((b, s // ts, n_heads, ts, HEAD_DIM), out_dtype)
        out_spec = pl.BlockSpec((1, 1, n_heads, ts, HEAD_DIM), lambda bi, i: (bi, i, 0, 0, 0))
    else:
        out_shape = jax.ShapeDtypeStruct(x.shape, out_dtype)
        out_spec = pl.BlockSpec((1, ts, w), lambda bi, i: (bi, i, 0))
    return pl.pallas_call(
        functools.partial(_rope_kernel, n_heads=n_heads, head_major=bool(head_major_tile)), name="rope",
        grid=(b, s // ts),
        in_specs=[pl.BlockSpec((1, ts, w), lambda bi, i: (bi, i, 0)),
                  pl.BlockSpec((1, ts, 1), lambda bi, i: (bi, i, 0)),
                  pl.BlockSpec((1, HEAD_DIM), lambda bi, i: (0, 0))],
        out_specs=out_spec, out_shape=out_shape,
        compiler_params=_params(("parallel", "parallel")))(x, pos, inv_lane)


def _index_kernel(ik_ref, iq_ref, w_ref, o_ref, x_ref, *, tq, tk, rc):
    qi = pl.program_id(1)
    kj = pl.program_id(2)
    needed = kj * tk <= qi * tq + tq - 1

    @pl.when(needed)
    def _():
        x_ref[...] = lax.dot_general(ik_ref[0], iq_ref[0, 0], (((1,), (1,)), ((), ())),
                                     preferred_element_type=F32)
        w = w_ref[0, 0] * (IDX_HEADS ** -0.5 * IDX_DIM ** -0.5)

        def chunk(r, carry):
            rows = pl.ds(pl.multiple_of(r * rc, rc), rc)
            acc = jnp.zeros((rc, tq), F32)
            for h in range(IDX_HEADS):
                acc = acc + jnp.maximum(x_ref[rows, h * tq:(h + 1) * tq], 0.0) * w[h:h + 1, :]
            o_ref[0, rows, :] = acc
            return carry

        lax.fori_loop(0, tk // rc, chunk, 0)

    @pl.when(jnp.logical_not(needed))
    def _():
        o_ref[...] = jnp.zeros_like(o_ref)


def index_scores(ik, iq_hm, w_hm, tq, tk):
    b, s, _ = ik.shape
    nq, nk = s // tq, s // tk

    def last_needed(qi):
        return (qi * tq + tq - 1) // tk

    return pl.pallas_call(
        functools.partial(_index_kernel, tq=tq, tk=tk, rc=min(tk, 32)), name="index_scores",
        grid=(b, nq, nk),
        in_specs=[pl.BlockSpec((1, tk, IDX_DIM), lambda bi, qi, kj: (bi, jnp.minimum(kj, last_needed(qi)), 0)),
                  pl.BlockSpec((1, 1, IDX_HEADS * tq, IDX_DIM), lambda bi, qi, kj: (bi, qi, 0, 0)),
                  pl.BlockSpec((1, 1, IDX_HEADS, tq), lambda bi, qi, kj: (bi, qi, 0, 0))],
        out_specs=pl.BlockSpec((1, tk, tq), lambda bi, qi, kj: (bi, kj, qi)),
        out_shape=jax.ShapeDtypeStruct((b, s, s), F32),
        scratch_shapes=[pltpu.VMEM((tk, IDX_HEADS * tq), F32)],
        compiler_params=_params(("parallel", "parallel", "arbitrary")))(ik, iq_hm, w_hm)


def _select_kernel(s_ref, idx_ref, key_ref, c_ref, acc_ref, *, tl, rb, pair, topk, row_stride):
    qi = pl.program_id(1)
    t_lane = qi * tl + lax.broadcasted_iota(I32, (1, tl), 1)
    nblk = ((qi + 1) * tl + rb - 1) // rb
    sub = rb // SUBLANES

    def rows_of(r):
        return pl.ds(pl.multiple_of(r * rb, rb), rb)

    def row_ids(r):
        return r * rb + lax.broadcasted_iota(I32, (rb, tl), 0)

    def fold(m):
        return jnp.sum(m.reshape(sub, SUBLANES, tl), axis=0)

    def make_keys(r, carry):
        x = s_ref[0, rows_of(r), :]
        x = jnp.where(x == 0.0, 0.0, x)
        bits = lax.bitcast_convert_type(x, I32)
        key = bits ^ ((bits >> 31) & 0x7FFFFFFF)
        key_ref[rows_of(r), :] = jnp.where(row_ids(r) <= t_lane, key, INT_MIN)
        return carry

    ngrp = (nblk + pair - 1) // pair
    lax.fori_loop(0, ngrp * pair, make_keys, 0)

    def count_rows(pred):
        def body(g, acc):
            for u in range(pair):
                acc = acc + fold(pred(key_ref[rows_of(g * pair + u), :]).astype(I32))
            return acc
        acc = lax.fori_loop(0, ngrp, body, jnp.zeros((SUBLANES, tl), I32))
        return jnp.sum(acc, axis=0, keepdims=True)

    zero = jnp.zeros((1, tl), I32)
    thr = jnp.where(count_rows(lambda k: k >= zero) >= topk, zero, jnp.full((1, tl), INT_MIN, I32))

    def bit_step(i, thr):
        cand = thr | (jnp.int32(1) << (30 - i))
        return jnp.where(count_rows(lambda k: k >= cand) >= topk, cand, thr)

    thr = lax.fori_loop(0, 31, bit_step, thr)
    need = (topk - count_rows(lambda k: k > thr)).astype(F32)

    ri = lax.broadcasted_iota(I32, (rb, rb), 0)
    ci = lax.broadcasted_iota(I32, (rb, rb), 1)
    tri = (ci <= ri).astype(BF16)

    def prefix(g, carry_eq):
        for u in range(pair):
            r = g * pair + u
            key = key_ref[rows_of(r), :]
            eq = jnp.logical_and(key == thr, row_ids(r) <= t_lane)
            ceq = jnp.dot(tri, eq.astype(BF16), preferred_element_type=F32) + carry_eq
            sel = jnp.logical_or(key > thr, jnp.logical_and(eq, ceq <= need))
            c_ref[rows_of(r), :] = jnp.dot(tri, sel.astype(BF16), preferred_element_type=F32)
            carry_eq = carry_eq + jnp.sum(fold(eq.astype(F32)), axis=0, keepdims=True)
        return carry_eq

    zf = jnp.zeros((1, tl), F32)
    lax.fori_loop(0, ngrp, prefix, zf)

    window = min(topk, SLOT_WINDOW)
    acc_ref[...] = jnp.zeros_like(acc_ref)

    def block_slots(r, before):
        c = c_ref[rows_of(r), :]
        cnt = c[rb - 1:rb, :]
        before_i = before.astype(I32)
        base_row = (r * rb).astype(F32)
        start = jnp.minimum(jnp.min(before_i) // SUBLANES * SUBLANES, topk - window)
        last = jnp.max((before + cnt).astype(I32))
        kmax = jnp.max(cnt).astype(I32)

        def make_kth(rows, n_rows, first_row):
            jrow = first_row + lax.broadcasted_iota(I32, (n_rows, tl), 0)

            def kth(kq, carry):
                acc = acc_ref[rows, :]
                for u in range(K_UNROLL):
                    k = kq * K_UNROLL + u
                    kf = k.astype(F32)
                    offset = jnp.sum(fold((c <= kf).astype(F32)), axis=0, keepdims=True)
                    hit = jnp.logical_and(jrow == before_i + k, kf < cnt)
                    acc = jnp.where(hit, offset + base_row, acc)
                acc_ref[rows, :] = acc
                return carry

            return kth

        n_iter = (kmax + K_UNROLL - 1) // K_UNROLL

        def in_window():
            rows = pl.ds(pl.multiple_of(start, SUBLANES), window)
            lax.fori_loop(0, n_iter, make_kth(rows, window, start), 0)

        def everywhere():
            lax.fori_loop(0, n_iter, make_kth(slice(None), topk, 0), 0)

        lax.cond(last - start <= window, in_window, everywhere)
        return before + cnt

    lax.fori_loop(0, nblk, block_slots, zf)
    idx_ref[0] = jnp.minimum(acc_ref[...].T.astype(I32), s_ref.shape[1] - 1) * row_stride


def select_topk(s_t, topk, row_stride=1):
    b, s, _ = s_t.shape
    tl = _tile(s, LANES)
    rb = _tile(s, 256)
    pair = next(p for p in (4, 2, 1) if (s // rb) % p == 0)
    return pl.pallas_call(
        functools.partial(_select_kernel, tl=tl, rb=rb, pair=pair, topk=topk, row_stride=row_stride),
        name="select_topk",
        grid=(b, s // tl),
        in_specs=[pl.BlockSpec((1, s, tl), lambda bi, qi: (bi, 0, qi))],
        out_specs=pl.BlockSpec((1, tl, topk), lambda bi, qi: (bi, qi, 0)),
        out_shape=jax.ShapeDtypeStruct((b, s, topk), I32),
        scratch_shapes=[pltpu.VMEM((s, tl), I32), pltpu.VMEM((s, tl), F32), pltpu.VMEM((topk, tl), F32)],
        compiler_params=_params(("parallel", "parallel")))(s_t)


HI16 = -65536
QUERY_BATCH = 16
GATHER_UNROLL = 128


def _pack_kv_kernel(k_ref, v_ref, o_ref):
    kb = lax.bitcast_convert_type(k_ref[...].astype(BF16).astype(F32), I32)
    vb = lax.bitcast_convert_type(v_ref[...].astype(BF16).astype(F32), I32)
    o_ref[...] = (vb & HI16) | ((kb >> 16) & 0xFFFF)


def pack_kv(k, v):
    n, w = k.shape
    tr = _tile(n, 1024)
    spec = pl.BlockSpec((tr, w), lambda i: (i, 0))
    return pl.pallas_call(
        _pack_kv_kernel, name="pack_kv", grid=(n // tr,), in_specs=[spec, spec], out_specs=spec,
        out_shape=jax.ShapeDtypeStruct((n, w), I32),
        compiler_params=_params(("parallel",)))(k, v)


def _sparse_attn_kernel(idx_ref, q_ref, kv_ref, o_ref, st_ref, *, tq, topk):
    qt = pl.program_id(1)
    scale = HEAD_DIM ** -0.5
    keys_per_iter = min(topk, GATHER_UNROLL)
    rows_per_iter = keys_per_iter * N_KV_HEADS

    def per_query(qq, carry):
        def gather(i, c):
            base = pl.multiple_of(i * keys_per_iter, keys_per_iter)
            dst = st_ref.at[qq, pl.ds(pl.multiple_of(i * rows_per_iter, rows_per_iter), rows_per_iter)]
            for u in range(keys_per_iter):
                r = pl.multiple_of(idx_ref[0, 0, qq * topk + base + u], N_KV_HEADS)
                dst[u * N_KV_HEADS:(u + 1) * N_KV_HEADS, :] = kv_ref[0, pl.ds(r, N_KV_HEADS), :]
            return c

        return lax.fori_loop(0, topk // keys_per_iter, gather, carry)

    lax.fori_loop(0, tq, per_query, 0)

    slot = lax.broadcasted_iota(I32, (QUERY_BATCH, KV_GROUP, topk), 2)
    qoff = lax.broadcasted_iota(I32, (QUERY_BATCH, KV_GROUP, topk), 0)

    def attend(qb, carry):
        rows = pl.ds(pl.multiple_of(qb * QUERY_BATCH, QUERY_BATCH), QUERY_BATCH)
        valid = slot < jnp.minimum(qt * tq + qb * QUERY_BATCH + qoff + 1, topk)
        for h in range(N_KV_HEADS):
            x = st_ref[rows, pl.ds(h, topk, stride=N_KV_HEADS), :]
            kb = lax.bitcast_convert_type(x << 16, F32).astype(BF16)
            vb = lax.bitcast_convert_type(x & HI16, F32).astype(BF16)
            s = jnp.einsum('qgd,qjd->qgj', q_ref[0, rows, h], kb, preferred_element_type=F32) * scale
            s = jnp.where(valid, s, -jnp.inf)
            e = jnp.exp(s - jnp.max(s, axis=-1, keepdims=True))
            p = e / jnp.sum(e, axis=-1, keepdims=True)
            o_ref[0, rows, h] = jnp.einsum('qgj,qjd->qgd', p.astype(BF16), vb, preferred_element_type=F32)
        return carry

    lax.fori_loop(0, tq // QUERY_BATCH, attend, 0)


def sparse_attention(idx, q, kv, topk):
    b, s, _ = idx.shape
    tq = _tile(s, 32)
    nqt = s // tq
    idx = idx.reshape(b * nqt, 1, tq * topk)
    return pl.pallas_call(
        functools.partial(_sparse_attn_kernel, tq=tq, topk=topk), name="sparse_attention",
        grid=(b, nqt),
        in_specs=[pl.BlockSpec((1, 1, tq * topk), lambda bi, qt: (bi * nqt + qt, 0, 0), memory_space=pltpu.SMEM),
                  pl.BlockSpec((1, tq, N_KV_HEADS, KV_GROUP, HEAD_DIM), lambda bi, qt: (bi, qt, 0, 0, 0)),
                  pl.BlockSpec((1, s * N_KV_HEADS, HEAD_DIM), lambda bi, qt: (bi, 0, 0),
                               pipeline_mode=pl.Buffered(1))],
        out_specs=pl.BlockSpec((1, tq, N_KV_HEADS, KV_GROUP, HEAD_DIM), lambda bi, qt: (bi, qt, 0, 0, 0)),
        out_shape=jax.ShapeDtypeStruct((b, s, N_KV_HEADS, KV_GROUP, HEAD_DIM), F32),
        scratch_shapes=[pltpu.VMEM((tq, topk * N_KV_HEADS, HEAD_DIM), I32)],
        compiler_params=_params(("parallel", "arbitrary")))(idx, q, kv)


def dsa_mixer(xn, w_in, w_out, positions):
    b, s, d = xn.shape
    n = b * s
    q_cols = d
    kv_cols = N_KV_HEADS * HEAD_DIM
    iq_cols = IDX_HEADS * IDX_DIM
    o0 = 0
    bounds = []
    for width in (q_cols, kv_cols, kv_cols, iq_cols, IDX_DIM, IDX_HEADS):
        bounds.append((o0, o0 + width))
        o0 += width
    w_bf = w_in.astype(BF16)
    x2 = xn.reshape(n, d)
    q, k, v, iq, ik, iw = (matmul(x2, w_bf[:, lo:hi]).reshape(b, s, hi - lo) for lo, hi in bounds)

    inv = ROPE_THETA ** (-jnp.arange(0, ROT_DIM, 2, dtype=F32) / ROT_DIM)
    inv_lane = jnp.tile(inv, LANES // inv.shape[0]).reshape(1, LANES)
    pos = positions.reshape(b, s, 1)
    tq = _tile(s, 256)
    tk = _tile(s, 512)
    q_r = rope(q, pos, inv_lane, d // HEAD_DIM, BF16)
    k_r = rope(k, pos, inv_lane, N_KV_HEADS, F32)
    iq_hm = rope(iq, pos, inv_lane, IDX_HEADS, BF16, head_major_tile=tq)
    ik_r = rope(ik, pos, inv_lane, 1, BF16)
    w_hm = iw.reshape(b, s // tq, tq, IDX_HEADS).transpose(0, 1, 3, 2)

    topk = min(INDEX_TOPK, s // 4)
    s_t = index_scores(ik_r, iq_hm.reshape(b, s // tq, IDX_HEADS * tq, IDX_DIM), w_hm, tq, tk)
    idx = select_topk(s_t, topk, row_stride=N_KV_HEADS)
    kv = pack_kv(k_r.reshape(n, kv_cols), v.reshape(n, kv_cols)).reshape(b, s * N_KV_HEADS, HEAD_DIM)
    o = sparse_attention(idx, q_r.reshape(b, s, N_KV_HEADS, KV_GROUP, HEAD_DIM), kv, topk)
    return matmul(o.reshape(n, d), w_out.astype(BF16))


def memory_xattn(hn, mem_n, wq, wkv, wo, b, s):
    width = wq.shape[1]
    m = mem_n.shape[0] // b
    q = matmul(hn, wq.astype(BF16), out_dtype=BF16).reshape(b, s, width)
    kv = matmul(mem_n, wkv.astype(BF16), out_dtype=BF16)
    km = kv[:, :width].reshape(b, m, width)
    vm = kv[:, width:].reshape(b, m, width)
    o = xattn_core(q, km, vm)
    return matmul(o.reshape(b * s, width), wo.astype(BF16))


def kernel(x, mem, positions, norm_gains, mem_norm, pool_w, pool_scale, dsa_w_in, dsa_w_out,
           xattn_wq, xattn_wkv, xattn_wo, ffn_w_gate_up, ffn_w_down):
    b, s, d = x.shape
    n = b * s
    depth = norm_gains.shape[0]
    mem_n = rms_norm(mem.reshape(-1, d), mem_norm, out_dtype=BF16)
    h = x.reshape(n, d)
    for i in range(depth):
        g = norm_gains[i]
        if i % 2 == 0:
            a = rms_norm(h, g[0])
            a = pool_mixer(a.reshape(b, s, d), pool_w[i // 2], pool_scale[i // 2]).reshape(n, d)
        else:
            a = rms_norm(h, g[0], out_dtype=BF16)
            a = dsa_mixer(a.reshape(b, s, d), dsa_w_in[i // 2], dsa_w_out[i // 2], positions)
        h = rms_norm(a, g[1], res=h)
        c = memory_xattn(rms_norm(h, g[2], out_dtype=BF16), mem_n, xattn_wq[i], xattn_wkv[i], xattn_wo[i], b, s)
        h = rms_norm(c, g[3], res=h)
        act = swiglu_up(rms_norm(h, g[4], out_dtype=BF16), ffn_w_gate_up[i].astype(BF16))
        f = matmul(act, ffn_w_down[i].astype(BF16))
        h = rms_norm(f, g[5], res=h)
    return h.reshape(b, s, d)
```

```python
import functools

import jax
import jax.numpy as jnp
from jax import lax
from jax.experimental import pallas as pl
from jax.experimental.pallas import tpu as pltpu

F32 = jnp.float32
BF16 = jnp.bfloat16
I32 = jnp.int32

LANES = 128
SUBLANES = 8
VMEM_LIMIT_BYTES = 56 * 1024 * 1024

NORM_EPS = 1e-6
POOL_WINDOWS = (2, 4, 8, 16)
MAX_WINDOW = 16
HEAD_DIM = 128
N_KV_HEADS = 4
KV_GROUP = 4
IDX_HEADS = 16
IDX_DIM = 128
INDEX_TOPK = 256
ROPE_THETA = 500000.0
ROT_DIM = 32
XATTN_HEADS = 4
XATTN_DIM = 128
INT_MIN = -(2 ** 31)
K_UNROLL = 4
SLOT_WINDOW = 64


def _params(semantics):
    return pltpu.CompilerParams(dimension_semantics=semantics, vmem_limit_bytes=VMEM_LIMIT_BYTES)


def _tile(n, want):
    t = min(n, want)
    while n % t:
        t //= 2
    return t


def _norm_body(x_ref, g_ref):
    x = x_ref[...].astype(F32)
    ms = jnp.mean(x * x, axis=-1, keepdims=True)
    return x * lax.rsqrt(ms + NORM_EPS) * g_ref[...]


def _norm_kernel(x_ref, g_ref, o_ref):
    o_ref[...] = _norm_body(x_ref, g_ref).astype(o_ref.dtype)


def _norm_res_kernel(x_ref, g_ref, r_ref, o_ref):
    o_ref[...] = (r_ref[...] + _norm_body(x_ref, g_ref)).astype(o_ref.dtype)


def rms_norm(x, g, res=None, out_dtype=F32):
    n, d = x.shape
    tr = _tile(n, 256)
    row = pl.BlockSpec((tr, d), lambda i: (i, 0))
    gain = pl.BlockSpec((1, d), lambda i: (0, 0))
    g2 = g.reshape(1, d).astype(F32)
    if res is None:
        kern, specs, args = _norm_kernel, [row, gain], (x, g2)
    else:
        kern, specs, args = _norm_res_kernel, [row, gain, row], (x, g2, res)
    return pl.pallas_call(
        kern, name="rms_norm", grid=(n // tr,), in_specs=specs, out_specs=row,
        out_shape=jax.ShapeDtypeStruct((n, d), out_dtype),
        compiler_params=_params(("parallel",)))(*args)


def _mm_kernel(a_ref, b_ref, o_ref):
    o_ref[...] = jnp.dot(a_ref[...].astype(BF16), b_ref[...].astype(BF16),
                         preferred_element_type=F32).astype(o_ref.dtype)


def matmul(a, b, out_dtype=F32, tm=512, tn=512):
    m, kd = a.shape
    n = b.shape[1]
    tm, tn = _tile(m, tm), _tile(n, tn)
    return pl.pallas_call(
        _mm_kernel, name="matmul",
        grid=(m // tm, n // tn),
        in_specs=[pl.BlockSpec((tm, kd), lambda i, j: (i, 0)),
                  pl.BlockSpec((kd, tn), lambda i, j: (0, j))],
        out_specs=pl.BlockSpec((tm, tn), lambda i, j: (i, j)),
        out_shape=jax.ShapeDtypeStruct((m, n), out_dtype),
        compiler_params=_params(("parallel", "parallel")))(a, b)


def _swiglu_kernel(a_ref, bg_ref, bu_ref, o_ref):
    a = a_ref[...].astype(BF16)
    g = jnp.dot(a, bg_ref[...].astype(BF16), preferred_element_type=F32)
    u = jnp.dot(a, bu_ref[...].astype(BF16), preferred_element_type=F32)
    o_ref[...] = (g * (1.0 / (1.0 + jnp.exp(-g))) * u).astype(o_ref.dtype)


def swiglu_up(a, w_gate_up, tm=512, tn=512):
    m, kd = a.shape
    f = w_gate_up.shape[1] // 2
    tm, tn = _tile(m, tm), _tile(f, tn)
    nf = f // tn
    return pl.pallas_call(
        _swiglu_kernel, name="swiglu_up",
        grid=(m // tm, nf),
        in_specs=[pl.BlockSpec((tm, kd), lambda i, j: (i, 0)),
                  pl.BlockSpec((kd, tn), lambda i, j: (0, j)),
                  pl.BlockSpec((kd, tn), lambda i, j: (0, j + nf))],
        out_specs=pl.BlockSpec((tm, tn), lambda i, j: (i, j)),
        out_shape=jax.ShapeDtypeStruct((m, f), BF16),
        compiler_params=_params(("parallel", "parallel")))(a, w_gate_up, w_gate_up)


def _pool_kernel(x_ref, halo_ref, w_ref, scale_ref, o_ref, pad_ref, *, ts, cg):
    i = pl.program_id(1)
    row = i * ts + lax.broadcasted_iota(I32, (ts, 1), 0)
    for g, win in enumerate(POOL_WINDOWS):
        cols = slice(g * cg, (g + 1) * cg)
        x = x_ref[0, :, cols]
        halo = halo_ref[0, :, cols]
        pad_ref[0:MAX_WINDOW, :] = jnp.where(i == 0, jnp.zeros_like(halo), halo)
        pad_ref[MAX_WINDOW:, :] = x
        wsum = x
        for k in range(1, win):
            wsum = wsum + pad_ref[MAX_WINDOW - k:MAX_WINDOW - k + ts, :]
        cnt = jnp.minimum(row + 1, win).astype(F32)
        p = wsum / cnt - x
        y = jnp.dot(p.astype(BF16), w_ref[g].astype(BF16), preferred_element_type=F32)
        o_ref[0, :, cols] = y * scale_ref[:, cols]


def pool_mixer(xn, w_group, scale):
    b, s, d = xn.shape
    g, cg, _ = w_group.shape
    ts = _tile(s, 512)
    hb = ts // MAX_WINDOW
    return pl.pallas_call(
        functools.partial(_pool_kernel, ts=ts, cg=cg), name="pool_mixer",
        grid=(b, s // ts),
        in_specs=[pl.BlockSpec((1, ts, d), lambda bi, i: (bi, i, 0)),
                  pl.BlockSpec((1, MAX_WINDOW, d), lambda bi, i: (bi, jnp.maximum(i * hb - 1, 0), 0)),
                  pl.BlockSpec((g, cg, cg), lambda bi, i: (0, 0, 0)),
                  pl.BlockSpec((1, d), lambda bi, i: (0, 0))],
        out_specs=pl.BlockSpec((1, ts, d), lambda bi, i: (bi, i, 0)),
        out_shape=jax.ShapeDtypeStruct((b, s, d), F32),
        scratch_shapes=[pltpu.VMEM((MAX_WINDOW + ts, cg), F32)],
        compiler_params=_params(("parallel", "arbitrary")))(xn, xn, w_group.astype(BF16), scale.reshape(1, d))


def _xattn_kernel(q_ref, k_ref, v_ref, o_ref):
    scale = XATTN_DIM ** -0.5
    for h in range(XATTN_HEADS):
        cols = slice(h * XATTN_DIM, (h + 1) * XATTN_DIM)
        q = q_ref[0, :, cols].astype(BF16)
        k = k_ref[0, :, cols].astype(BF16)
        v = v_ref[0, :, cols].astype(BF16)
        s = lax.dot_general(q, k, (((1,), (1,)), ((), ())), preferred_element_type=F32) * scale
        e = jnp.exp(s - jnp.max(s, axis=-1, keepdims=True))
        p = e / jnp.sum(e, axis=-1, keepdims=True)
        o_ref[0, :, cols] = jnp.dot(p.astype(BF16), v, preferred_element_type=F32).astype(o_ref.dtype)


def xattn_core(q, km, vm):
    b, s, w = q.shape
    m = km.shape[1]
    ts = _tile(s, 512)
    return pl.pallas_call(
        _xattn_kernel, name="xattn_core",
        grid=(b, s // ts),
        in_specs=[pl.BlockSpec((1, ts, w), lambda bi, i: (bi, i, 0)),
                  pl.BlockSpec((1, m, w), lambda bi, i: (bi, 0, 0)),
                  pl.BlockSpec((1, m, w), lambda bi, i: (bi, 0, 0))],
        out_specs=pl.BlockSpec((1, ts, w), lambda bi, i: (bi, i, 0)),
        out_shape=jax.ShapeDtypeStruct((b, s, w), BF16),
        compiler_params=_params(("parallel", "parallel")))(q, km, vm)


def _rope_kernel(x_ref, pos_ref, inv_ref, o_ref, *, n_heads, head_major):
    ts = x_ref.shape[1]
    lane = lax.broadcasted_iota(I32, (ts, HEAD_DIM), 1)
    half = ROT_DIM // 2
    ang = pos_ref[0].astype(F32) * inv_ref[...]
    sn = jnp.sin(ang)
    cos_t = jnp.where(lane < ROT_DIM, jnp.cos(ang), 1.0)
    sin_t = jnp.where(lane < half, -sn, jnp.where(lane < ROT_DIM, sn, 0.0))
    for h in range(n_heads):
        x = x_ref[0, :, h * HEAD_DIM:(h + 1) * HEAD_DIM].astype(F32)
        partner = jnp.where(lane < half, pltpu.roll(x, LANES - half, 1), pltpu.roll(x, half, 1))
        y = (x * cos_t + partner * sin_t).astype(o_ref.dtype)
        if head_major:
            o_ref[0, 0, h] = y
        else:
            o_ref[0, :, h * HEAD_DIM:(h + 1) * HEAD_DIM] = y


def rope(x, pos, inv_lane, n_heads, out_dtype, head_major_tile=None):
    b, s, w = x.shape
    ts = head_major_tile or _tile(s, 256)
    if head_major_tile:
        out_shape = jax.ShapeDtypeStruct((b, s // ts, n_heads, ts, HEAD_DIM), out_dtype)
        out_spec = pl.BlockSpec((1, 1, n_heads, ts, HEAD_DIM), lambda bi, i: (bi, i, 0, 0, 0))
    else:
        out_shape = jax.ShapeDtypeStruct(x.shape, out_dtype)
        out_spec = pl.BlockSpec((1, ts, w), lambda bi, i: (bi, i, 0))
    return pl.pallas_call(
        functools.partial(_rope_kernel, n_heads=n_heads, head_major=bool(head_major_tile)), name="rope",
        grid=(b, s // ts),
        in_specs=[pl.BlockSpec((1, ts, w), lambda bi, i: (bi, i, 0)),
                  pl.BlockSpec((1, ts, 1), lambda bi, i: (bi, i, 0)),
                  pl.BlockSpec((1, HEAD_DIM), lambda bi, i: (0, 0))],
        out_specs=out_spec, out_shape=out_shape,
        compiler_params=_params(("parallel", "parallel")))(x, pos, inv_lane)


def _index_kernel(ik_ref, iq_ref, w_ref, o_ref, xa_ref, xb_ref, *, tq, tk, rc):
    qi = pl.program_id(1)
    kj = pl.program_id(2)
    needed = kj * tk <= qi * tq + tq - 1
    half = tk // 2

    @pl.when(needed)
    def _():
        w = w_ref[0, 0] * (IDX_HEADS ** -0.5 * IDX_DIM ** -0.5)
        contract_last = (((1,), (1,)), ((), ()))

        def reduce_heads(x_ref, row0):
            for r in range(half // rc):
                acc = jnp.zeros((rc, tq), F32)
                for h in range(IDX_HEADS):
                    acc = acc + jnp.maximum(x_ref[r * rc:(r + 1) * rc, h * tq:(h + 1) * tq], 0.0) * w[h:h + 1, :]
                o_ref[0, row0 + r * rc:row0 + (r + 1) * rc, :] = acc

        xa_ref[...] = lax.dot_general(ik_ref[0, :half], iq_ref[0, 0], contract_last,
                                      preferred_element_type=F32)
        xb_ref[...] = lax.dot_general(ik_ref[0, half:], iq_ref[0, 0], contract_last,
                                      preferred_element_type=F32)
        reduce_heads(xa_ref, 0)
        reduce_heads(xb_ref, half)

    @pl.when(jnp.logical_not(needed))
    def _():
        o_ref[...] = jnp.zeros_like(o_ref)


def index_scores(ik, iq_hm, w_hm, tq, tk):
    b, s, _ = ik.shape
    nq, nk = s // tq, s // tk

    def last_needed(qi):
        return (qi * tq + tq - 1) // tk

    return pl.pallas_call(
        functools.partial(_index_kernel, tq=tq, tk=tk, rc=min(tk, 32)), name="index_scores",
        grid=(b, nq, nk),
        in_specs=[pl.BlockSpec((1, tk, IDX_DIM), lambda bi, qi, kj: (bi, jnp.minimum(kj, last_needed(qi)), 0)),
                  pl.BlockSpec((1, 1, IDX_HEADS * tq, IDX_DIM), lambda bi, qi, kj: (bi, qi, 0, 0)),
                  pl.BlockSpec((1, 1, IDX_HEADS, tq), lambda bi, qi, kj: (bi, qi, 0, 0))],
        out_specs=pl.BlockSpec((1, tk, tq), lambda bi, qi, kj: (bi, kj, qi)),
        out_shape=jax.ShapeDtypeStruct((b, s, s), F32),
        scratch_shapes=[pltpu.VMEM((tk // 2, IDX_HEADS * tq), F32), pltpu.VMEM((tk // 2, IDX_HEADS * tq), F32)],
        compiler_params=_params(("parallel", "parallel", "arbitrary")))(ik, iq_hm, w_hm)


def _select_kernel(s_ref, idx_ref, key_ref, c_ref, acc_ref, *, tl, rb, pair, topk, row_stride):
    qi = pl.program_id(1)
    t_lane = qi * tl + lax.broadcasted_iota(I32, (1, tl), 1)
    nblk = ((qi + 1) * tl + rb - 1) // rb
    sub = rb // SUBLANES

    def rows_of(r):
        return pl.ds(pl.multiple_of(r * rb, rb), rb)

    def row_ids(r):
        return r * rb + lax.broadcasted_iota(I32, (rb, tl), 0)

    def fold(m):
        return jnp.sum(m.reshape(sub, SUBLANES, tl), axis=0)

    def make_keys(r, carry):
        x = s_ref[0, rows_of(r), :]
        x = jnp.where(x == 0.0, 0.0, x)
        bits = lax.bitcast_convert_type(x, I32)
        key = bits ^ ((bits >> 31) & 0x7FFFFFFF)
        key_ref[rows_of(r), :] = jnp.where(row_ids(r) <= t_lane, key, INT_MIN)
        return carry

    ngrp = (nblk + pair - 1) // pair
    lax.fori_loop(0, ngrp * pair, make_keys, 0)

    def count_rows(pred):
        def body(g, acc):
            for u in range(pair):
                acc = acc + fold(pred(key_ref[rows_of(g * pair + u), :]).astype(I32))
            return acc
        acc = lax.fori_loop(0, ngrp, body, jnp.zeros((SUBLANES, tl), I32))
        return jnp.sum(acc, axis=0, keepdims=True)

    zero = jnp.zeros((1, tl), I32)
    thr = jnp.where(count_rows(lambda k: k >= zero) >= topk, zero, jnp.full((1, tl), INT_MIN, I32))

    def bit_step(i, thr):
        cand = thr | (jnp.int32(1) << (30 - i))
        return jnp.where(count_rows(lambda k: k >= cand) >= topk, cand, thr)

    thr = lax.fori_loop(0, 31, bit_step, thr)
    need = (topk - count_rows(lambda k: k > thr)).astype(F32)

    ri = lax.broadcasted_iota(I32, (rb, rb), 0)
    ci = lax.broadcasted_iota(I32, (rb, rb), 1)
    tri = (ci <= ri).astype(BF16)

    def prefix(g, carry_eq):
        for u in range(pair):
            r = g * pair + u
            key = key_ref[rows_of(r), :]
            eq = jnp.logical_and(key == thr, row_ids(r) <= t_lane)
            ceq = jnp.dot(tri, eq.astype(BF16), preferred_element_type=F32) + carry_eq
            sel = jnp.logical_or(key > thr, jnp.logical_and(eq, ceq <= need))
            c_ref[rows_of(r), :] = jnp.dot(tri, sel.astype(BF16), preferred_element_type=F32)
            carry_eq = carry_eq + jnp.sum(fold(eq.astype(F32)), axis=0, keepdims=True)
        return carry_eq

    zf = jnp.zeros((1, tl), F32)
    lax.fori_loop(0, ngrp, prefix, zf)

    window = min(topk, SLOT_WINDOW)
    acc_ref[...] = jnp.zeros_like(acc_ref)

    def block_slots(r, before):
        c = c_ref[rows_of(r), :]
        cnt = c[rb - 1:rb, :]
        before_i = before.astype(I32)
        base_row = (r * rb).astype(F32)
        start = jnp.minimum(jnp.min(before_i) // SUBLANES * SUBLANES, topk - window)
        last = jnp.max((before + cnt).astype(I32))
        kmax = jnp.max(cnt).astype(I32)

        def make_kth(rows, n_rows, first_row):
            jrow = first_row + lax.broadcasted_iota(I32, (n_rows, tl), 0)

            def kth(kq, carry):
                acc = acc_ref[rows, :]
                for u in range(K_UNROLL):
                    k = kq * K_UNROLL + u
                    kf = k.astype(F32)
                    offset = jnp.sum(fold((c <= kf).astype(F32)), axis=0, keepdims=True)
                    hit = jnp.logical_and(jrow == before_i + k, kf < cnt)
                    acc = jnp.where(hit, offset + base_row, acc)
                acc_ref[rows, :] = acc
                return carry

            return kth

        n_iter = (kmax + K_UNROLL - 1) // K_UNROLL

        def in_window():
            rows = pl.ds(pl.multiple_of(start, SUBLANES), window)
            lax.fori_loop(0, n_iter, make_kth(rows, window, start), 0)

        def everywhere():
            lax.fori_loop(0, n_iter, make_kth(slice(None), topk, 0), 0)

        lax.cond(last - start <= window, in_window, everywhere)
        return before + cnt

    lax.fori_loop(0, nblk, block_slots, zf)
    idx_ref[0] = jnp.minimum(acc_ref[...].T.astype(I32), s_ref.shape[1] - 1) * row_stride


def select_topk(s_t, topk, row_stride=1):
    b, s, _ = s_t.shape
    tl = _tile(s, LANES)
    rb = _tile(s, 256)
    pair = next(p for p in (4, 2, 1) if (s // rb) % p == 0)
    return pl.pallas_call(
        functools.partial(_select_kernel, tl=tl, rb=rb, pair=pair, topk=topk, row_stride=row_stride),
        name="select_topk",
        grid=(b, s // tl),
        in_specs=[pl.BlockSpec((1, s, tl), lambda bi, qi: (bi, 0, qi))],
        out_specs=pl.BlockSpec((1, tl, topk), lambda bi, qi: (bi, qi, 0)),
        out_shape=jax.ShapeDtypeStruct((b, s, topk), I32),
        scratch_shapes=[pltpu.VMEM((s, tl), I32), pltpu.VMEM((s, tl), F32), pltpu.VMEM((topk, tl), F32)],
        compiler_params=_params(("parallel", "parallel")))(s_t)


HI16 = -65536
QUERY_BATCH = 16
GATHER_UNROLL = 128


def _pack_kv_kernel(k_ref, v_ref, o_ref):
    kb = lax.bitcast_convert_type(k_ref[...].astype(BF16).astype(F32), I32)
    vb = lax.bitcast_convert_type(v_ref[...].astype(BF16).astype(F32), I32)
    o_ref[...] = (vb & HI16) | ((kb >> 16) & 0xFFFF)


def pack_kv(k, v):
    n, w = k.shape
    tr = _tile(n, 1024)
    spec = pl.BlockSpec((tr, w), lambda i: (i, 0))
    return pl.pallas_call(
        _pack_kv_kernel, name="pack_kv", grid=(n // tr,), in_specs=[spec, spec], out_specs=spec,
        out_shape=jax.ShapeDtypeStruct((n, w), I32),
        compiler_params=_params(("parallel",)))(k, v)


def _sparse_attn_kernel(idx_ref, q_ref, kv_ref, o_ref, st_ref, *, tq, topk):
    qt = pl.program_id(1)
    scale = HEAD_DIM ** -0.5
    keys_per_iter = min(topk, GATHER_UNROLL)
    rows_per_iter = keys_per_iter * N_KV_HEADS

    def per_query(qq, carry):
        def gather(i, c):
            base = pl.multiple_of(i * keys_per_iter, keys_per_iter)
            dst = st_ref.at[qq, pl.ds(pl.multiple_of(i * rows_per_iter, rows_per_iter), rows_per_iter)]
            for u in range(keys_per_iter):
                r = pl.multiple_of(idx_ref[0, 0, qq * topk + base + u], N_KV_HEADS)
                dst[u * N_KV_HEADS:(u + 1) * N_KV_HEADS, :] = kv_ref[0, pl.ds(r, N_KV_HEADS), :]
            return c

        return lax.fori_loop(0, topk // keys_per_iter, gather, carry)

    lax.fori_loop(0, tq, per_query, 0)

    slot = lax.broadcasted_iota(I32, (QUERY_BATCH, KV_GROUP, topk), 2)
    qoff = lax.broadcasted_iota(I32, (QUERY_BATCH, KV_GROUP, topk), 0)

    def attend(qb, carry):
        rows = pl.ds(pl.multiple_of(qb * QUERY_BATCH, QUERY_BATCH), QUERY_BATCH)
        valid = slot < jnp.minimum(qt * tq + qb * QUERY_BATCH + qoff + 1, topk)
        for h in range(N_KV_HEADS):
            x = st_ref[rows, pl.ds(h, topk, stride=N_KV_HEADS), :]
            kb = lax.bitcast_convert_type(x << 16, F32).astype(BF16)
            vb = lax.bitcast_convert_type(x & HI16, F32).astype(BF16)
            s = jnp.einsum('qgd,qjd->qgj', q_ref[0, rows, h], kb, preferred_element_type=F32) * scale
            s = jnp.where(valid, s, -jnp.inf)
            e = jnp.exp(s - jnp.max(s, axis=-1, keepdims=True))
            p = e / jnp.sum(e, axis=-1, keepdims=True)
            o_ref[0, rows, h] = jnp.einsum('qgj,qjd->qgd', p.astype(BF16), vb, preferred_element_type=F32)
        return carry

    lax.fori_loop(0, tq // QUERY_BATCH, attend, 0)


def sparse_attention(idx, q, kv, topk):
    b, s, _ = idx.shape
    tq = _tile(s, 32)
    nqt = s // tq
    idx = idx.reshape(b * nqt, 1, tq * topk)
    return pl.pallas_call(
        functools.partial(_sparse_attn_kernel, tq=tq, topk=topk), name="sparse_attention",
        grid=(b, nqt),
        in_specs=[pl.BlockSpec((1, 1, tq * topk), lambda bi, qt: (bi * nqt + qt, 0, 0), memory_space=pltpu.SMEM),
                  pl.BlockSpec((1, tq, N_KV_HEADS, KV_GROUP, HEAD_DIM), lambda bi, qt: (bi, qt, 0, 0, 0)),
                  pl.BlockSpec((1, s * N_KV_HEADS, HEAD_DIM), lambda bi, qt: (bi, 0, 0),
                               pipeline_mode=pl.Buffered(1))],
        out_specs=pl.BlockSpec((1, tq, N_KV_HEADS, KV_GROUP, HEAD_DIM), lambda bi, qt: (bi, qt, 0, 0, 0)),
        out_shape=jax.ShapeDtypeStruct((b, s, N_KV_HEADS, KV_GROUP, HEAD_DIM), F32),
        scratch_shapes=[pltpu.VMEM((tq, topk * N_KV_HEADS, HEAD_DIM), I32)],
        compiler_params=_params(("parallel", "arbitrary")))(idx, q, kv)


def dsa_mixer(xn, w_in, w_out, positions):
    b, s, d = xn.shape
    n = b * s
    q_cols = d
    kv_cols = N_KV_HEADS * HEAD_DIM
    iq_cols = IDX_HEADS * IDX_DIM
    o0 = 0
    bounds = []
    for width in (q_cols, kv_cols, kv_cols, iq_cols, IDX_DIM, IDX_HEADS):
        bounds.append((o0, o0 + width))
        o0 += width
    w_bf = w_in.astype(BF16)
    x2 = xn.reshape(n, d)
    q, k, v, iq, ik, iw = (matmul(x2, w_bf[:, lo:hi]).reshape(b, s, hi - lo) for lo, hi in bounds)

    inv = ROPE_THETA ** (-jnp.arange(0, ROT_DIM, 2, dtype=F32) / ROT_DIM)
    inv_lane = jnp.tile(inv, LANES // inv.shape[0]).reshape(1, LANES)
    pos = positions.reshape(b, s, 1)
    tq = _tile(s, 256)
    tk = _tile(s, 512)
    q_r = rope(q, pos, inv_lane, d // HEAD_DIM, BF16)
    k_r = rope(k, pos, inv_lane, N_KV_HEADS, F32)
    iq_hm = rope(iq, pos, inv_lane, IDX_HEADS, BF16, head_major_tile=tq)
    ik_r = rope(ik, pos, inv_lane, 1, BF16)
    w_hm = iw.reshape(b, s // tq, tq, IDX_HEADS).transpose(0, 1, 3, 2)

    topk = min(INDEX_TOPK, s // 4)
    s_t = index_scores(ik_r, iq_hm.reshape(b, s // tq, IDX_HEADS * tq, IDX_DIM), w_hm, tq, tk)
    idx = select_topk(s_t, topk, row_stride=N_KV_HEADS)
    kv = pack_kv(k_r.reshape(n, kv_cols), v.reshape(n, kv_cols)).reshape(b, s * N_KV_HEADS, HEAD_DIM)
    o = sparse_attention(idx, q_r.reshape(b, s, N_KV_HEADS, KV_GROUP, HEAD_DIM), kv, topk)
    return matmul(o.reshape(n, d), w_out.astype(BF16))


def memory_xattn(hn, mem_n, wq, wkv, wo, b, s):
    width = wq.shape[1]
    m = mem_n.shape[0] // b
    q = matmul(hn, wq.astype(BF16), out_dtype=BF16).reshape(b, s, width)
    kv = matmul(mem_n, wkv.astype(BF16), out_dtype=BF16)
    km = kv[:, :width].reshape(b, m, width)
    vm = kv[:, width:].reshape(b, m, width)
    o = xattn_core(q, km, vm)
    return matmul(o.reshape(b * s, width), wo.astype(BF16))


def kernel(x, mem, positions, norm_gains, mem_norm, pool_w, pool_scale, dsa_w_in, dsa_w_out,
           xattn_wq, xattn_wkv, xattn_wo, ffn_w_gate_up, ffn_w_down):
    b, s, d = x.shape
    n = b * s
    depth = norm_gains.shape[0]
    mem_n = rms_norm(mem.reshape(-1, d), mem_norm, out_dtype=BF16)
    h = x.reshape(n, d)
    for i in range(depth):
        g = norm_gains[i]
        if i % 2 == 0:
            a = rms_norm(h, g[0])
            a = pool_mixer(a.reshape(b, s, d), pool_w[i // 2], pool_scale[i // 2]).reshape(n, d)
        else:
            a = rms_norm(h, g[0], out_dtype=BF16)
            a = dsa_mixer(a.reshape(b, s, d), dsa_w_in[i // 2], dsa_w_out[i // 2], positions)
        h = rms_norm(a, g[1], res=h)
        c = memory_xattn(rms_norm(h, g[2], out_dtype=BF16), mem_n, xattn_wq[i], xattn_wkv[i], xattn_wo[i], b, s)
        h = rms_norm(c, g[3], res=h)
        act = swiglu_up(rms_norm(h, g[4], out_dtype=BF16), ffn_w_gate_up[i].astype(BF16))
        f = matmul(act, ffn_w_down[i].astype(BF16))
        h = rms_norm(f, g[5], res=h)
    return h.reshape(b, s, d)
```

```python
import functools

import jax
import jax.numpy as jnp
from jax import lax
from jax.experimental import pallas as pl
from jax.experimental.pallas import tpu as pltpu

F32 = jnp.float32
BF16 = jnp.bfloat16
I32 = jnp.int32

LANES = 128
SUBLANES = 8
VMEM_LIMIT_BYTES = 56 * 1024 * 1024

NORM_EPS = 1e-6
POOL_WINDOWS = (2, 4, 8, 16)
MAX_WINDOW = 16
HEAD_DIM = 128
N_KV_HEADS = 4
KV_GROUP = 4
IDX_HEADS = 16
IDX_DIM = 128
INDEX_TOPK = 256
ROPE_THETA = 500000.0
ROT_DIM = 32
XATTN_HEADS = 4
XATTN_DIM = 128
INT_MIN = -(2 ** 31)
INDEX_PART_ROWS = 256
K_UNROLL = 4
SLOT_WINDOW = 64


def _params(semantics):
    return pltpu.CompilerParams(dimension_semantics=semantics, vmem_limit_bytes=VMEM_LIMIT_BYTES)


def _tile(n, want):
    t = min(n, want)
    while n % t:
        t //= 2
    return t


def _norm_body(x_ref, g_ref):
    x = x_ref[...].astype(F32)
    ms = jnp.mean(x * x, axis=-1, keepdims=True)
    return x * lax.rsqrt(ms + NORM_EPS) * g_ref[...]


def _norm_kernel(x_ref, g_ref, o_ref):
    o_ref[...] = _norm_body(x_ref, g_ref).astype(o_ref.dtype)


def _norm_res_kernel(x_ref, g_ref, r_ref, o_ref):
    o_ref[...] = (r_ref[...] + _norm_body(x_ref, g_ref)).astype(o_ref.dtype)


def rms_norm(x, g, res=None, out_dtype=F32):
    n, d = x.shape
    tr = _tile(n, 256)
    row = pl.BlockSpec((tr, d), lambda i: (i, 0))
    gain = pl.BlockSpec((1, d), lambda i: (0, 0))
    g2 = g.reshape(1, d).astype(F32)
    if res is None:
        kern, specs, args = _norm_kernel, [row, gain], (x, g2)
    else:
        kern, specs, args = _norm_res_kernel, [row, gain, row], (x, g2, res)
    return pl.pallas_call(
        kern, name="rms_norm", grid=(n // tr,), in_specs=specs, out_specs=row,
        out_shape=jax.ShapeDtypeStruct((n, d), out_dtype),
        compiler_params=_params(("parallel",)))(*args)


def _mm_kernel(a_ref, b_ref, o_ref):
    o_ref[...] = jnp.dot(a_ref[...].astype(BF16), b_ref[...].astype(BF16),
                         preferred_element_type=F32).astype(o_ref.dtype)


def matmul(a, b, out_dtype=F32, tm=512, tn=512):
    m, kd = a.shape
    n = b.shape[1]
    tm, tn = _tile(m, tm), _tile(n, tn)
    return pl.pallas_call(
        _mm_kernel, name="matmul",
        grid=(m // tm, n // tn),
        in_specs=[pl.BlockSpec((tm, kd), lambda i, j: (i, 0)),
                  pl.BlockSpec((kd, tn), lambda i, j: (0, j))],
        out_specs=pl.BlockSpec((tm, tn), lambda i, j: (i, j)),
        out_shape=jax.ShapeDtypeStruct((m, n), out_dtype),
        compiler_params=_params(("parallel", "parallel")))(a, b)


def _swiglu_kernel(a_ref, bg_ref, bu_ref, o_ref):
    a = a_ref[...].astype(BF16)
    g = jnp.dot(a, bg_ref[...].astype(BF16), preferred_element_type=F32)
    u = jnp.dot(a, bu_ref[...].astype(BF16), preferred_element_type=F32)
    o_ref[...] = (g * (1.0 / (1.0 + jnp.exp(-g))) * u).astype(o_ref.dtype)


def swiglu_up(a, w_gate_up, tm=512, tn=512):
    m, kd = a.shape
    f = w_gate_up.shape[1] // 2
    tm, tn = _tile(m, tm), _tile(f, tn)
    nf = f // tn
    return pl.pallas_call(
        _swiglu_kernel, name="swiglu_up",
        grid=(m // tm, nf),
        in_specs=[pl.BlockSpec((tm, kd), lambda i, j: (i, 0)),
                  pl.BlockSpec((kd, tn), lambda i, j: (0, j)),
                  pl.BlockSpec((kd, tn), lambda i, j: (0, j + nf))],
        out_specs=pl.BlockSpec((tm, tn), lambda i, j: (i, j)),
        out_shape=jax.ShapeDtypeStruct((m, f), BF16),
        compiler_params=_params(("parallel", "parallel")))(a, w_gate_up, w_gate_up)


def _pool_kernel(x_ref, halo_ref, w_ref, scale_ref, o_ref, pad_ref, *, ts, cg):
    i = pl.program_id(1)
    row = i * ts + lax.broadcasted_iota(I32, (ts, 1), 0)
    for g, win in enumerate(POOL_WINDOWS):
        cols = slice(g * cg, (g + 1) * cg)
        x = x_ref[0, :, cols]
        halo = halo_ref[0, :, cols]
        pad_ref[0:MAX_WINDOW, :] = jnp.where(i == 0, jnp.zeros_like(halo), halo)
        pad_ref[MAX_WINDOW:, :] = x
        wsum = x
        for k in range(1, win):
            wsum = wsum + pad_ref[MAX_WINDOW - k:MAX_WINDOW - k + ts, :]
        cnt = jnp.minimum(row + 1, win).astype(F32)
        p = wsum / cnt - x
        y = jnp.dot(p.astype(BF16), w_ref[g].astype(BF16), preferred_element_type=F32)
        o_ref[0, :, cols] = y * scale_ref[:, cols]


def pool_mixer(xn, w_group, scale):
    b, s, d = xn.shape
    g, cg, _ = w_group.shape
    ts = _tile(s, 512)
    hb = ts // MAX_WINDOW
    return pl.pallas_call(
        functools.partial(_pool_kernel, ts=ts, cg=cg), name="pool_mixer",
        grid=(b, s // ts),
        in_specs=[pl.BlockSpec((1, ts, d), lambda bi, i: (bi, i, 0)),
                  pl.BlockSpec((1, MAX_WINDOW, d), lambda bi, i: (bi, jnp.maximum(i * hb - 1, 0), 0)),
                  pl.BlockSpec((g, cg, cg), lambda bi, i: (0, 0, 0)),
                  pl.BlockSpec((1, d), lambda bi, i: (0, 0))],
        out_specs=pl.BlockSpec((1, ts, d), lambda bi, i: (bi, i, 0)),
        out_shape=jax.ShapeDtypeStruct((b, s, d), F32),
        scratch_shapes=[pltpu.VMEM((MAX_WINDOW + ts, cg), F32)],
        compiler_params=_params(("parallel", "arbitrary")))(xn, xn, w_group.astype(BF16), scale.reshape(1, d))


def _xattn_kernel(q_ref, k_ref, v_ref, o_ref):
    scale = XATTN_DIM ** -0.5
    for h in range(XATTN_HEADS):
        cols = slice(h * XATTN_DIM, (h + 1) * XATTN_DIM)
        q = q_ref[0, :, cols].astype(BF16)
        k = k_ref[0, :, cols].astype(BF16)
        v = v_ref[0, :, cols].astype(BF16)
        s = lax.dot_general(q, k, (((1,), (1,)), ((), ())), preferred_element_type=F32) * scale
        e = jnp.exp(s - jnp.max(s, axis=-1, keepdims=True))
        p = e / jnp.sum(e, axis=-1, keepdims=True)
        o_ref[0, :, cols] = jnp.dot(p.astype(BF16), v, preferred_element_type=F32).astype(o_ref.dtype)


def xattn_core(q, km, vm):
    b, s, w = q.shape
    m = km.shape[1]
    ts = _tile(s, 512)
    return pl.pallas_call(
        _xattn_kernel, name="xattn_core",
        grid=(b, s // ts),
        in_specs=[pl.BlockSpec((1, ts, w), lambda bi, i: (bi, i, 0)),
                  pl.BlockSpec((1, m, w), lambda bi, i: (bi, 0, 0)),
                  pl.BlockSpec((1, m, w), lambda bi, i: (bi, 0, 0))],
        out_specs=pl.BlockSpec((1, ts, w), lambda bi, i: (bi, i, 0)),
        out_shape=jax.ShapeDtypeStruct((b, s, w), BF16),
        compiler_params=_params(("parallel", "parallel")))(q, km, vm)


def _rope_kernel(x_ref, pos_ref, inv_ref, o_ref, *, n_heads, head_major):
    ts = x_ref.shape[1]
    lane = lax.broadcasted_iota(I32, (ts, HEAD_DIM), 1)
    half = ROT_DIM // 2
    ang = pos_ref[0].astype(F32) * inv_ref[...]
    sn = jnp.sin(ang)
    cos_t = jnp.where(lane < ROT_DIM, jnp.cos(ang), 1.0)
    sin_t = jnp.where(lane < half, -sn, jnp.where(lane < ROT_DIM, sn, 0.0))
    for h in range(n_heads):
        x = x_ref[0, :, h * HEAD_DIM:(h + 1) * HEAD_DIM].astype(F32)
        partner = jnp.where(lane < half, pltpu.roll(x, LANES - half, 1), pltpu.roll(x, half, 1))
        y = (x * cos_t + partner * sin_t).astype(o_ref.dtype)
        if head_major:
            o_ref[0, 0, h] = y
        else:
            o_ref[0, :, h * HEAD_DIM:(h + 1) * HEAD_DIM] = y


def rope(x, pos, inv_lane, n_heads, out_dtype, head_major_tile=None):
    b, s, w = x.shape
    ts = head_major_tile or _tile(s, 256)
    if head_major_tile:
        out_shape = jax.ShapeDtypeStruct((b, s // ts, n_heads, ts, HEAD_DIM), out_dtype)
        out_spec = pl.BlockSpec((1, 1, n_heads, ts, HEAD_DIM), lambda bi, i: (bi, i, 0, 0, 0))
    else:
        out_shape = jax.ShapeDtypeStruct(x.shape, out_dtype)
        out_spec = pl.BlockSpec((1, ts, w), lambda bi, i: (bi, i, 0))
    return pl.pallas_call(
        functools.partial(_rope_kernel, n_heads=n_heads, head_major=bool(head_major_tile)), name="rope",
        grid=(b, s // ts),
        in_specs=[pl.BlockSpec((1, ts, w), lambda bi, i: (bi, i, 0)),
                  pl.BlockSpec((1, ts, 1), lambda bi, i: (bi, i, 0)),
                  pl.BlockSpec((1, HEAD_DIM), lambda bi, i: (0, 0))],
        out_specs=out_spec, out_shape=out_shape,
        compiler_params=_params(("parallel", "parallel")))(x, pos, inv_lane)


def _index_kernel(ik_ref, iq_ref, w_ref, o_ref, *x_refs, tq, tk, rc):
    qi = pl.program_id(1)
    kj = pl.program_id(2)
    needed = kj * tk <= qi * tq + tq - 1
    half = tk // len(x_refs)

    @pl.when(needed)
    def _():
        w = w_ref[0, 0] * (IDX_HEADS ** -0.5 * IDX_DIM ** -0.5)
        contract_last = (((1,), (1,)), ((), ()))

        def reduce_heads(x_ref, row0):
            for r in range(half // rc):
                acc = jnp.zeros((rc, tq), F32)
                for h in range(IDX_HEADS):
                    acc = acc + jnp.maximum(x_ref[r * rc:(r + 1) * rc, h * tq:(h + 1) * tq], 0.0) * w[h:h + 1, :]
                o_ref[0, row0 + r * rc:row0 + (r + 1) * rc, :] = acc

        for p, x_ref in enumerate(x_refs):
            x_ref[...] = lax.dot_general(ik_ref[0, p * half:(p + 1) * half], iq_ref[0, 0], contract_last,
                                         preferred_element_type=F32)
        for p, x_ref in enumerate(x_refs):
            reduce_heads(x_ref, p * half)

    @pl.when(jnp.logical_not(needed))
    def _():
        o_ref[...] = jnp.zeros_like(o_ref)


def index_scores(ik, iq_hm, w_hm, tq, tk):
    b, s, _ = ik.shape
    nq, nk = s // tq, s // tk

    def last_needed(qi):
        return (qi * tq + tq - 1) // tk

    return pl.pallas_call(
        functools.partial(_index_kernel, tq=tq, tk=tk, rc=min(tk, 32)), name="index_scores",
        grid=(b, nq, nk),
        in_specs=[pl.BlockSpec((1, tk, IDX_DIM), lambda bi, qi, kj: (bi, jnp.minimum(kj, last_needed(qi)), 0)),
                  pl.BlockSpec((1, 1, IDX_HEADS * tq, IDX_DIM), lambda bi, qi, kj: (bi, qi, 0, 0)),
                  pl.BlockSpec((1, 1, IDX_HEADS, tq), lambda bi, qi, kj: (bi, qi, 0, 0))],
        out_specs=pl.BlockSpec((1, tk, tq), lambda bi, qi, kj: (bi, kj, qi)),
        out_shape=jax.ShapeDtypeStruct((b, s, s), F32),
        scratch_shapes=[pltpu.VMEM((min(tk, INDEX_PART_ROWS), IDX_HEADS * tq), F32)] * max(1, tk // INDEX_PART_ROWS),
        compiler_params=_params(("parallel", "parallel", "arbitrary")))(ik, iq_hm, w_hm)


def _select_kernel(s_ref, idx_ref, key_ref, c_ref, acc_ref, *, tl, rb, pair, topk, row_stride):
    qi = pl.program_id(1)
    t_lane = qi * tl + lax.broadcasted_iota(I32, (1, tl), 1)
    nblk = ((qi + 1) * tl + rb - 1) // rb
    sub = rb // SUBLANES

    def rows_of(r):
        return pl.ds(pl.multiple_of(r * rb, rb), rb)

    def row_ids(r):
        return r * rb + lax.broadcasted_iota(I32, (rb, tl), 0)

    def fold(m):
        return jnp.sum(m.reshape(sub, SUBLANES, tl), axis=0)

    def make_keys(r, carry):
        x = s_ref[0, rows_of(r), :]
        x = jnp.where(x == 0.0, 0.0, x)
        bits = lax.bitcast_convert_type(x, I32)
        key = bits ^ ((bits >> 31) & 0x7FFFFFFF)
        key_ref[rows_of(r), :] = jnp.where(row_ids(r) <= t_lane, key, INT_MIN)
        return carry

    ngrp = (nblk + pair - 1) // pair
    lax.fori_loop(0, ngrp * pair, make_keys, 0)

    def count_rows(pred):
        def body(g, acc):
            for u in range(pair):
                acc = acc + fold(pred(key_ref[rows_of(g * pair + u), :]).astype(I32))
            return acc
        acc = lax.fori_loop(0, ngrp, body, jnp.zeros((SUBLANES, tl), I32))
        return jnp.sum(acc, axis=0, keepdims=True)

    zero = jnp.zeros((1, tl), I32)
    thr = jnp.where(count_rows(lambda k: k >= zero) >= topk, zero, jnp.full((1, tl), INT_MIN, I32))

    def bit_step(i, thr):
        cand = thr | (jnp.int32(1) << (30 - i))
        return jnp.where(count_rows(lambda k: k >= cand) >= topk, cand, thr)

    thr = lax.fori_loop(0, 31, bit_step, thr)
    need = (topk - count_rows(lambda k: k > thr)).astype(F32)

    ri = lax.broadcasted_iota(I32, (rb, rb), 0)
    ci = lax.broadcasted_iota(I32, (rb, rb), 1)
    tri = (ci <= ri).astype(BF16)

    def prefix(g, carry_eq):
        for u in range(pair):
            r = g * pair + u
            key = key_ref[rows_of(r), :]
            eq = jnp.logical_and(key == thr, row_ids(r) <= t_lane)
            ceq = jnp.dot(tri, eq.astype(BF16), preferred_element_type=F32) + carry_eq
            sel = jnp.logical_or(key > thr, jnp.logical_and(eq, ceq <= need))
            c_ref[rows_of(r), :] = jnp.dot(tri, sel.astype(BF16), preferred_element_type=F32)
            carry_eq = carry_eq + jnp.sum(fold(eq.astype(F32)), axis=0, keepdims=True)
        return carry_eq

    zf = jnp.zeros((1, tl), F32)
    lax.fori_loop(0, ngrp, prefix, zf)

    window = min(topk, SLOT_WINDOW)
    acc_ref[...] = jnp.zeros_like(acc_ref)

    def block_slots(r, before):
        c = c_ref[rows_of(r), :]
        cnt = c[rb - 1:rb, :]
        before_i = before.astype(I32)
        base_row = (r * rb).astype(F32)
        start = jnp.minimum(jnp.min(before_i) // SUBLANES * SUBLANES, topk - window)
        last = jnp.max((before + cnt).astype(I32))
        kmax = jnp.max(cnt).astype(I32)

        def make_kth(rows, n_rows, first_row):
            jrow = first_row + lax.broadcasted_iota(I32, (n_rows, tl), 0)

            def kth(kq, carry):
                acc = acc_ref[rows, :]
                for u in range(K_UNROLL):
                    k = kq * K_UNROLL + u
                    kf = k.astype(F32)
                    offset = jnp.sum(fold((c <= kf).astype(F32)), axis=0, keepdims=True)
                    hit = jnp.logical_and(jrow == before_i + k, kf < cnt)
                    acc = jnp.where(hit, offset + base_row, acc)
                acc_ref[rows, :] = acc
                return carry

            return kth

        n_iter = (kmax + K_UNROLL - 1) // K_UNROLL

        def in_window():
            rows = pl.ds(pl.multiple_of(start, SUBLANES), window)
            lax.fori_loop(0, n_iter, make_kth(rows, window, start), 0)

        def everywhere():
            lax.fori_loop(0, n_iter, make_kth(slice(None), topk, 0), 0)

        lax.cond(last - start <= window, in_window, everywhere)
        return before + cnt

    lax.fori_loop(0, nblk, block_slots, zf)
    idx_ref[0] = jnp.minimum(acc_ref[...].T.astype(I32), s_ref.shape[1] - 1) * row_stride


def select_topk(s_t, topk, row_stride=1):
    b, s, _ = s_t.shape
    tl = _tile(s, LANES)
    rb = _tile(s, 256)
    pair = next(p for p in (4, 2, 1) if (s // rb) % p == 0)
    return pl.pallas_call(
        functools.partial(_select_kernel, tl=tl, rb=rb, pair=pair, topk=topk, row_stride=row_stride),
        name="select_topk",
        grid=(b, s // tl),
        in_specs=[pl.BlockSpec((1, s, tl), lambda bi, qi: (bi, 0, qi))],
        out_specs=pl.BlockSpec((1, tl, topk), lambda bi, qi: (bi, qi, 0)),
        out_shape=jax.ShapeDtypeStruct((b, s, topk), I32),
        scratch_shapes=[pltpu.VMEM((s, tl), I32), pltpu.VMEM((s, tl), F32), pltpu.VMEM((topk, tl), F32)],
        compiler_params=_params(("parallel", "parallel")))(s_t)


HI16 = -65536
QUERY_BATCH = 16
GATHER_UNROLL = 128


def _pack_kv_kernel(k_ref, v_ref, o_ref):
    kb = lax.bitcast_convert_type(k_ref[...].astype(BF16).astype(F32), I32)
    vb = lax.bitcast_convert_type(v_ref[...].astype(BF16).astype(F32), I32)
    o_ref[...] = (vb & HI16) | ((kb >> 16) & 0xFFFF)


def pack_kv(k, v):
    n, w = k.shape
    tr = _tile(n, 1024)
    spec = pl.BlockSpec((tr, w), lambda i: (i, 0))
    return pl.pallas_call(
        _pack_kv_kernel, name="pack_kv", grid=(n // tr,), in_specs=[spec, spec], out_specs=spec,
        out_shape=jax.ShapeDtypeStruct((n, w), I32),
        compiler_params=_params(("parallel",)))(k, v)


def _sparse_attn_kernel(idx_ref, q_ref, kv_ref, o_ref, st_ref, *, tq, topk):
    qt = pl.program_id(1)
    scale = HEAD_DIM ** -0.5
    keys_per_iter = min(topk, GATHER_UNROLL)
    rows_per_iter = keys_per_iter * N_KV_HEADS

    def per_query(qq, carry):
        def gather(i, c):
            base = pl.multiple_of(i * keys_per_iter, keys_per_iter)
            dst = st_ref.at[qq, pl.ds(pl.multiple_of(i * rows_per_iter, rows_per_iter), rows_per_iter)]
            for u in range(keys_per_iter):
                r = pl.multiple_of(idx_ref[0, 0, qq * topk + base + u], N_KV_HEADS)
                dst[u * N_KV_HEADS:(u + 1) * N_KV_HEADS, :] = kv_ref[0, pl.ds(r, N_KV_HEADS), :]
            return c

        return lax.fori_loop(0, topk // keys_per_iter, gather, carry)

    lax.fori_loop(0, tq, per_query, 0)

    slot = lax.broadcasted_iota(I32, (QUERY_BATCH, KV_GROUP, topk), 2)
    qoff = lax.broadcasted_iota(I32, (QUERY_BATCH, KV_GROUP, topk), 0)

    def attend(qb, carry):
        rows = pl.ds(pl.multiple_of(qb * QUERY_BATCH, QUERY_BATCH), QUERY_BATCH)
        valid = slot < jnp.minimum(qt * tq + qb * QUERY_BATCH + qoff + 1, topk)
        for h in range(N_KV_HEADS):
            x = st_ref[rows, pl.ds(h, topk, stride=N_KV_HEADS), :]
            kb = lax.bitcast_convert_type(x << 16, F32).astype(BF16)
            vb = lax.bitcast_convert_type(x & HI16, F32).astype(BF16)
            s = jnp.einsum('qgd,qjd->qgj', q_ref[0, rows, h], kb, preferred_element_type=F32) * scale
            s = jnp.where(valid, s, -jnp.inf)
            e = jnp.exp(s - jnp.max(s, axis=-1, keepdims=True))
            p = e / jnp.sum(e, axis=-1, keepdims=True)
            o_ref[0, rows, h] = jnp.einsum('qgj,qjd->qgd', p.astype(BF16), vb, preferred_element_type=F32)
        return carry

    lax.fori_loop(0, tq // QUERY_BATCH, attend, 0)


def sparse_attention(idx, q, kv, topk):
    b, s, _ = idx.shape
    tq = _tile(s, 32)
    nqt = s // tq
    idx = idx.reshape(b * nqt, 1, tq * topk)
    return pl.pallas_call(
        functools.partial(_sparse_attn_kernel, tq=tq, topk=topk), name="sparse_attention",
        grid=(b, nqt),
        in_specs=[pl.BlockSpec((1, 1, tq * topk), lambda bi, qt: (bi * nqt + qt, 0, 0), memory_space=pltpu.SMEM),
                  pl.BlockSpec((1, tq, N_KV_HEADS, KV_GROUP, HEAD_DIM), lambda bi, qt: (bi, qt, 0, 0, 0)),
                  pl.BlockSpec((1, s * N_KV_HEADS, HEAD_DIM), lambda bi, qt: (bi, 0, 0),
                               pipeline_mode=pl.Buffered(1))],
        out_specs=pl.BlockSpec((1, tq, N_KV_HEADS, KV_GROUP, HEAD_DIM), lambda bi, qt: (bi, qt, 0, 0, 0)),
        out_shape=jax.ShapeDtypeStruct((b, s, N_KV_HEADS, KV_GROUP, HEAD_DIM), F32),
        scratch_shapes=[pltpu.VMEM((tq, topk * N_KV_HEADS, HEAD_DIM), I32)],
        compiler_params=_params(("parallel", "arbitrary")))(idx, q, kv)


def dsa_mixer(xn, w_in, w_out, positions):
    b, s, d = xn.shape
    n = b * s
    q_cols = d
    kv_cols = N_KV_HEADS * HEAD_DIM
    iq_cols = IDX_HEADS * IDX_DIM
    o0 = 0
    bounds = []
    for width in (q_cols, kv_cols, kv_cols, iq_cols, IDX_DIM, IDX_HEADS):
        bounds.append((o0, o0 + width))
        o0 += width
    w_bf = w_in.astype(BF16)
    x2 = xn.reshape(n, d)
    q, k, v, iq, ik, iw = (matmul(x2, w_bf[:, lo:hi]).reshape(b, s, hi - lo) for lo, hi in bounds)

    inv = ROPE_THETA ** (-jnp.arange(0, ROT_DIM, 2, dtype=F32) / ROT_DIM)
    inv_lane = jnp.tile(inv, LANES // inv.shape[0]).reshape(1, LANES)
    pos = positions.reshape(b, s, 1)
    tq = _tile(s, 256)
    tk = _tile(s, 1024)
    q_r = rope(q, pos, inv_lane, d // HEAD_DIM, BF16)
    k_r = rope(k, pos, inv_lane, N_KV_HEADS, F32)
    iq_hm = rope(iq, pos, inv_lane, IDX_HEADS, BF16, head_major_tile=tq)
    ik_r = rope(ik, pos, inv_lane, 1, BF16)
    w_hm = iw.reshape(b, s // tq, tq, IDX_HEADS).transpose(0, 1, 3, 2)

    topk = min(INDEX_TOPK, s // 4)
    s_t = index_scores(ik_r, iq_hm.reshape(b, s // tq, IDX_HEADS * tq, IDX_DIM), w_hm, tq, tk)
    idx = select_topk(s_t, topk, row_stride=N_KV_HEADS)
    kv = pack_kv(k_r.reshape(n, kv_cols), v.reshape(n, kv_cols)).reshape(b, s * N_KV_HEADS, HEAD_DIM)
    o = sparse_attention(idx, q_r.reshape(b, s, N_KV_HEADS, KV_GROUP, HEAD_DIM), kv, topk)
    return matmul(o.reshape(n, d), w_out.astype(BF16))


def memory_xattn(hn, mem_n, wq, wkv, wo, b, s):
    width = wq.shape[1]
    m = mem_n.shape[0] // b
    q = matmul(hn, wq.astype(BF16), out_dtype=BF16).reshape(b, s, width)
    kv = matmul(mem_n, wkv.astype(BF16), out_dtype=BF16)
    km = kv[:, :width].reshape(b, m, width)
    vm = kv[:, width:].reshape(b, m, width)
    o = xattn_core(q, km, vm)
    return matmul(o.reshape(b * s, width), wo.astype(BF16))


def kernel(x, mem, positions, norm_gains, mem_norm, pool_w, pool_scale, dsa_w_in, dsa_w_out,
           xattn_wq, xattn_wkv, xattn_wo, ffn_w_gate_up, ffn_w_down):
    b, s, d = x.shape
    n = b * s
    depth = norm_gains.shape[0]
    mem_n = rms_norm(mem.reshape(-1, d), mem_norm, out_dtype=BF16)
    h = x.reshape(n, d)
    for i in range(depth):
        g = norm_gains[i]
        if i % 2 == 0:
            a = rms_norm(h, g[0])
            a = pool_mixer(a.reshape(b, s, d), pool_w[i // 2], pool_scale[i // 2]).reshape(n, d)
        else:
            a = rms_norm(h, g[0], out_dtype=BF16)
            a = dsa_mixer(a.reshape(b, s, d), dsa_w_in[i // 2], dsa_w_out[i // 2], positions)
        h = rms_norm(a, g[1], res=h)
        c = memory_xattn(rms_norm(h, g[2], out_dtype=BF16), mem_n, xattn_wq[i], xattn_wkv[i], xattn_wo[i], b, s)
        h = rms_norm(c, g[3], res=h)
        act = swiglu_up(rms_norm(h, g[4], out_dtype=BF16), ffn_w_gate_up[i].astype(BF16))
        f = matmul(act, ffn_w_down[i].astype(BF16))
        h = rms_norm(f, g[5], res=h)
    return h.reshape(b, s, d)
```
